```python
import jax
import jax.numpy as jnp
from jax import lax
import numpy as np

D_MODEL = 1024
BATCH = 16
SEQ = 2048
DEPTH = 2

GRID_W = 64
CTX_LEN = 256
HEAD_DIM = 64
ROPE_BASE = 10000.0
EPS = 1e-6
NEG_INF = -1e30

ATT_HEADS = 8
ATT_KV_HEADS = 2
ATT_WIDTH = ATT_HEADS * HEAD_DIM
KV_WIDTH = ATT_KV_HEADS * HEAD_DIM
WINDOW = 128
BLOCK = 128

RET_HEADS = 4
RET_WIDTH = RET_HEADS * HEAD_DIM
RET_CHUNK = 128

CONV_WIDTH = D_MODEL - ATT_WIDTH - RET_WIDTH
CONV_KERNEL = 31
CONV_PAD = CONV_KERNEL // 2

MIX_WIDTH = ATT_WIDTH + RET_WIDTH + CONV_WIDTH
PROJ_SIZES = (ATT_WIDTH, KV_WIDTH, KV_WIDTH, RET_WIDTH, RET_WIDTH, RET_WIDTH, RET_WIDTH, 2 * CONV_WIDTH)
IN_COLS = ATT_WIDTH + 2 * KV_WIDTH + 4 * RET_WIDTH + 2 * CONV_WIDTH

N_EXPERTS = 16
CAPACITY_FACTOR = 2
D_FF = 2816

kernel_name = "hybrid_dit_window_gqa_retention_conformer_ecmoe"


def rms_norm(x, g):
    xf = x.astype(jnp.float32)
    y = xf * lax.rsqrt(jnp.mean(xf * xf, axis=-1, keepdims=True) + EPS)
    return (y * g.astype(jnp.float32)).astype(x.dtype)


def modulate(h, shift, scale):
    return h * (1 + scale) + shift


def heads(t, n):
    return t.reshape(t.shape[:-1] + (n, HEAD_DIM))


def split_proj(u):
    cuts = [int(s) for s in np.cumsum(PROJ_SIZES)[:-1]]
    return jnp.split(u, cuts, axis=-1)


def rotary(x, pos):
    half = x.shape[-1] // 2
    inv = ROPE_BASE ** (-jnp.arange(half, dtype=jnp.float32) / half)
    ang = pos[:, None] * inv[None, :]
    cos = jnp.cos(ang)[:, None, :]
    sin = jnp.sin(ang)[:, None, :]
    xf = x.astype(jnp.float32)
    x1, x2 = xf[..., :half], xf[..., half:]
    return jnp.concatenate([x1 * cos - x2 * sin, x1 * sin + x2 * cos], axis=-1).astype(x.dtype)


def axial_rotary(x, rows, cols):
    half = x.shape[-1] // 2
    return jnp.concatenate([rotary(x[..., :half], rows), rotary(x[..., half:], cols)], axis=-1)


def window_attention(q, k, v, k_ctx, v_ctx, sink):
    B, S, H, d = q.shape
    G = H // ATT_KV_HEADS
    nb = S // BLOCK
    W3 = 3 * BLOCK
    C = k_ctx.shape[1]
    scale = d ** -0.5
    qb = q.reshape(B, nb, BLOCK, ATT_KV_HEADS, G, d)
    pad = ((0, 0), (BLOCK, BLOCK), (0, 0), (0, 0))
    kp = jnp.pad(k, pad).reshape(B, nb + 2, BLOCK, ATT_KV_HEADS, d)
    vp = jnp.pad(v, pad).reshape(B, nb + 2, BLOCK, ATT_KV_HEADS, d)
    kw = jnp.concatenate([kp[:, :-2], kp[:, 1:-1], kp[:, 2:]], axis=2)
    vw = jnp.concatenate([vp[:, :-2], vp[:, 1:-1], vp[:, 2:]], axis=2)
    s_loc = jnp.einsum('bnqkgd,bnwkd->bnkgqw', qb, kw).astype(jnp.float32) * scale
    s_ctx = jnp.einsum('bnqkgd,bckd->bnkgqc', qb, k_ctx).astype(jnp.float32) * scale
    a = jnp.arange(BLOCK)[:, None]
    w = jnp.arange(W3)[None, :]
    kpos = jnp.arange(nb)[:, None, None] * BLOCK - BLOCK + w[None]
    valid = (jnp.abs(w - BLOCK - a)[None] <= WINDOW) & (kpos >= 0) & (kpos < S)
    s_loc = jnp.where(valid[None, :, None, None], s_loc, NEG_INF)
    sink_b = jnp.broadcast_to(sink.astype(jnp.float32).reshape(1, 1, ATT_KV_HEADS, G, 1, 1),
                              s_loc.shape[:-1] + (1,))
    p = jax.nn.softmax(jnp.concatenate([s_loc, s_ctx, sink_b], axis=-1), axis=-1).astype(v.dtype)
    out = (jnp.einsum('bnkgqw,bnwkd->bnqkgd', p[..., :W3], vw)
           + jnp.einsum('bnkgqc,bckd->bnqkgd', p[..., W3:W3 + C], v_ctx))
    return out.reshape(B, S, H * d)


def context_attention(q, k, v, sink):
    B, C, H, d = q.shape
    G = H // ATT_KV_HEADS
    qg = q.reshape(B, C, ATT_KV_HEADS, G, d)
    s = jnp.einsum('bqkgd,bckd->bkgqc', qg, k).astype(jnp.float32) * (d ** -0.5)
    sink_b = jnp.broadcast_to(sink.astype(jnp.float32).reshape(1, ATT_KV_HEADS, G, 1, 1), s.shape[:-1] + (1,))
    p = jax.nn.softmax(jnp.concatenate([s, sink_b], axis=-1), axis=-1)[..., :C].astype(v.dtype)
    return jnp.einsum('bkgqc,bckd->bqkgd', p, v).reshape(B, C, H * d)


def retention_dir(q, k, v, log_gamma, s0, inclusive):
    B, L, H, d = q.shape
    T = RET_CHUNK
    n = L // T
    qc = q.reshape(B, n, T, H, d)
    kc = k.reshape(B, n, T, H, d)
    vc = v.reshape(B, n, T, H, d)
    i = jnp.arange(T, dtype=jnp.float32)
    diff = i[:, None] - i[None, :]
    mask = (diff >= 0) if inclusive else (diff > 0)
    dec = jnp.where(mask[None], jnp.exp(log_gamma[:, None, None] * jnp.maximum(diff, 0.0)[None]), 0.0)
    s = jnp.einsum('bnihd,bnjhd->bnhij', qc, kc) * dec[None, None]
    o_intra = jnp.einsum('bnhij,bnjhd->bnihd', s, vc)
    k_dec = jnp.exp((T - 1 - i)[:, None] * log_gamma[None, :])
    kv = jnp.einsum('bnjhd,jh,bnjhe->bnhde', kc, k_dec, vc)
    chunk_decay = jnp.exp(log_gamma * T)[None, :, None, None]

    def step(state, kv_n):
        return chunk_decay * state + kv_n, state

    s_final, s_prev = lax.scan(step, s0, jnp.moveaxis(kv, 1, 0))
    s_prev = jnp.moveaxis(s_prev, 0, 1)
    q_dec = jnp.exp((i + 1.0)[:, None] * log_gamma[None, :])
    o_inter = jnp.einsum('bnihd,ih,bnhde->bnihe', qc, q_dec, s_prev)
    return (o_intra + o_inter).reshape(B, L, H, d), s_final


def retention_bidir(q_c, k_c, v_c, q_l, k_l, v_l, log_g_f, log_g_b):
    f32 = jnp.float32
    q_c, k_c, v_c, q_l, k_l, v_l = (t.astype(f32) for t in (q_c, k_c, v_c, q_l, k_l, v_l))
    B, _, H, d = q_l.shape
    zeros = jnp.zeros((B, H, d, d), f32)
    rev = lambda t: t[:, ::-1]
    o_cf, s_cf = retention_dir(q_c, k_c, v_c, log_g_f, zeros, True)
    o_cb, s_cb = retention_dir(rev(q_c), rev(k_c), rev(v_c), log_g_b, zeros, False)
    o_lf, _ = retention_dir(q_l, k_l, v_l, log_g_f, s_cf, True)
    o_lb, _ = retention_dir(rev(q_l), rev(k_l), rev(v_l), log_g_b, s_cb, False)
    return o_cf + rev(o_cb), o_lf + rev(o_lb)


def retention_out(o, gate, g):
    B, L, H, d = o.shape
    mu = jnp.mean(o, axis=-1, keepdims=True)
    var = jnp.mean(jnp.square(o - mu), axis=-1, keepdims=True)
    y = ((o - mu) * lax.rsqrt(var + EPS)).reshape(B, L, H * d) * g.astype(jnp.float32)
    return y.astype(gate.dtype) * jax.nn.silu(gate)


def conformer_conv(u, w_dw, b_dw, g, b):
    val, gt = jnp.split(u, 2, axis=-1)
    h = val * jax.nn.sigmoid(gt)
    h = lax.conv_general_dilated(h, w_dw[:, None, :].astype(h.dtype), window_strides=(1,),
                                 padding=((CONV_PAD, CONV_PAD),),
                                 dimension_numbers=('NWC', 'WIO', 'NWC'),
                                 feature_group_count=CONV_WIDTH) + b_dw
    hf = h.astype(jnp.float32)
    mu = jnp.mean(hf, axis=-1, keepdims=True)
    var = jnp.mean(jnp.square(hf - mu), axis=-1, keepdims=True)
    y = (hf - mu) * lax.rsqrt(var + EPS) * g.astype(jnp.float32) + b.astype(jnp.float32)
    return jax.nn.silu(y).astype(u.dtype)


def expert_choice_ffn(h, w_router, w_gate, w_up, w_down):
    B, L, D = h.shape
    cap = CAPACITY_FACTOR * L // N_EXPERTS
    aff = jax.nn.softmax(jnp.einsum('bld,de->ble', h, w_router).astype(jnp.float32), axis=-1)
    g, idx = lax.top_k(jnp.swapaxes(aff, 1, 2), cap)
    xe = jax.vmap(lambda hb, ib: hb[ib])(h, idx)
    a = jnp.einsum('becd,edf->becf', xe, w_gate)
    u = jnp.einsum('becd,edf->becf', xe, w_up)
    y = jnp.einsum('becf,efd->becd', jax.nn.silu(a) * u, w_down) * g[..., None].astype(h.dtype)
    return jax.vmap(lambda ib, yb: jnp.zeros((L, D), yb.dtype).at[ib.reshape(-1)].add(yb.reshape(-1, D)))(idx, y)


def setup_inputs(seed: int = 0) -> dict:
    key = jax.random.key(seed)
    ks = jax.random.split(key, 24)
    f32 = jnp.float32
    nrm = lambda k, shape, s: jax.random.normal(k, shape, f32) * s
    D = D_MODEL
    gamma0 = 1.0 - 2.0 ** (-5.0 - jnp.arange(RET_HEADS, dtype=f32))
    return {
        "x": nrm(ks[0], (BATCH, SEQ, D), 1.0),
        "c": nrm(ks[1], (BATCH, D), 1.0),
        "ctx": nrm(ks[2], (BATCH, CTX_LEN, D), 1.0),
        "c_ctx": nrm(ks[3], (D,), 1.0),
        "w_mod": nrm(ks[4], (DEPTH, D, 6 * D), 0.5 * D ** -0.5),
        "b_mod": nrm(ks[5], (DEPTH, 6 * D), 0.02),
        "g_mix": 1.0 + nrm(ks[6], (DEPTH, D), 0.02),
        "g_ffn": 1.0 + nrm(ks[7], (DEPTH, D), 0.02),
        "w_in": nrm(ks[8], (DEPTH, D, IN_COLS), D ** -0.5),
        "q_norm_g": 1.0 + nrm(ks[9], (DEPTH, HEAD_DIM), 0.02),
        "k_norm_g": 1.0 + nrm(ks[10], (DEPTH, HEAD_DIM), 0.02),
        "att_sink": nrm(ks[11], (DEPTH, ATT_HEADS), 0.5),
        "ret_decay_logit": jnp.log(gamma0 / (1.0 - gamma0)) + nrm(ks[12], (DEPTH, 2, RET_HEADS), 0.1),
        "ret_norm_g": 1.0 + nrm(ks[13], (DEPTH, RET_WIDTH), 0.02),
        "conv_w": nrm(ks[14], (DEPTH, CONV_KERNEL, CONV_WIDTH), CONV_KERNEL ** -0.5),
        "conv_b": nrm(ks[15], (DEPTH, CONV_WIDTH), 0.02),
        "conv_norm_g": 1.0 + nrm(ks[16], (DEPTH, CONV_WIDTH), 0.02),
        "conv_norm_b": nrm(ks[17], (DEPTH, CONV_WIDTH), 0.02),
        "w_out": nrm(ks[18], (DEPTH, MIX_WIDTH, D), MIX_WIDTH ** -0.5),
        "w_router": nrm(ks[19], (DEPTH, D, N_EXPERTS), D ** -0.5),
        "w_gate": nrm(ks[20], (DEPTH, N_EXPERTS, D, D_FF), D ** -0.5),
        "w_up": nrm(ks[21], (DEPTH, N_EXPERTS, D, D_FF), D ** -0.5),
        "w_down": nrm(ks[22], (DEPTH, N_EXPERTS, D_FF, D), D_FF ** -0.5),
    }


def reference(x, c, ctx, c_ctx, w_mod, b_mod, g_mix, g_ffn, w_in, q_norm_g, k_norm_g, att_sink,
              ret_decay_logit, ret_norm_g, conv_w, conv_b, conv_norm_g, conv_norm_b, w_out,
              w_router, w_gate, w_up, w_down):
    f32 = jnp.float32
    B, S, D = x.shape
    C = ctx.shape[1]
    n_rows = S // GRID_W
    rows = jnp.repeat(jnp.arange(n_rows), GRID_W).astype(f32)
    cols = jnp.tile(jnp.arange(GRID_W), n_rows).astype(f32)
    pos_ctx = jnp.arange(C, dtype=f32)
    pos_lat = C + jnp.arange(S, dtype=f32)
    xc = ctx
    cs = jax.nn.silu(c)
    ccs = jax.nn.silu(c_ctx)
    for l in range(DEPTH):
        last = l == DEPTH - 1
        mod_l = (cs @ w_mod[l] + b_mod[l])[:, None, :]
        mod_c = (ccs @ w_mod[l] + b_mod[l])[None, None, :]
        sh_a, sc_a, gt_a, sh_f, sc_f, gt_f = jnp.split(mod_l, 6, axis=-1)
        csh_a, csc_a, cgt_a, csh_f, csc_f, cgt_f = jnp.split(mod_c, 6, axis=-1)

        h_l = modulate(rms_norm(x, g_mix[l]), sh_a, sc_a)
        h_c = modulate(rms_norm(xc, g_mix[l]), csh_a, csc_a)
        aq_l, ak_l, av_l, rq_l, rk_l, rv_l, rg_l, cv_l = split_proj(h_l @ w_in[l])
        aq_c, ak_c, av_c, rq_c, rk_c, rv_c, rg_c, cv_c = split_proj(h_c @ w_in[l])

        k_att_c = rms_norm(heads(ak_c, ATT_KV_HEADS), k_norm_g[l])
        v_att_c = heads(av_c, ATT_KV_HEADS)
        q_att_l = axial_rotary(rms_norm(heads(aq_l, ATT_HEADS), q_norm_g[l]), rows, cols)
        k_att_l = axial_rotary(rms_norm(heads(ak_l, ATT_KV_HEADS), k_norm_g[l]), rows, cols)
        att_l = window_attention(q_att_l, k_att_l, heads(av_l, ATT_KV_HEADS), k_att_c, v_att_c, att_sink[l])

        log_g = jax.nn.log_sigmoid(ret_decay_logit[l].astype(f32))
        kscale = HEAD_DIM ** -0.5
        o_c, o_l = retention_bidir(
            rotary(heads(rq_c, RET_HEADS), pos_ctx), rotary(heads(rk_c, RET_HEADS), pos_ctx) * kscale,
            heads(rv_c, RET_HEADS),
            rotary(heads(rq_l, RET_HEADS), pos_lat), rotary(heads(rk_l, RET_HEADS), pos_lat) * kscale,
            heads(rv_l, RET_HEADS), log_g[0], log_g[1])
        ret_l = retention_out(o_l, rg_l, ret_norm_g[l])

        conv_l = conformer_conv(cv_l, conv_w[l], conv_b[l], conv_norm_g[l], conv_norm_b[l])

        mix_l = jnp.concatenate([att_l, ret_l, conv_l], axis=-1) @ w_out[l]
        x_mid = x + gt_a * mix_l

        if not last:
            q_att_c = rms_norm(heads(aq_c, ATT_HEADS), q_norm_g[l])
            att_c = context_attention(q_att_c, k_att_c, v_att_c, att_sink[l])
            ret_c = retention_out(o_c, rg_c, ret_norm_g[l])
            conv_c = conformer_conv(cv_c, conv_w[l], conv_b[l], conv_norm_g[l], conv_norm_b[l])
            xc = xc + cgt_a * (jnp.concatenate([att_c, ret_c, conv_c], axis=-1) @ w_out[l])
            hf_c = modulate(rms_norm(xc, g_ffn[l]), csh_f, csc_f)
            xc = xc + cgt_f * expert_choice_ffn(hf_c, w_router[l], w_gate[l], w_up[l], w_down[l])

        hf_l = modulate(rms_norm(x_mid, g_ffn[l]), sh_f, sc_f)
        x = x_mid + gt_f * expert_choice_ffn(hf_l, w_router[l], w_gate[l], w_up[l], w_down[l])
    return x
```

```python
import functools

import numpy as np
import jax
import jax.numpy as jnp
from jax import lax
from jax.experimental import pallas as pl
from jax.experimental.pallas import tpu as pltpu

F32 = jnp.float32
BF16 = jnp.bfloat16
HIGHEST = lax.Precision.HIGHEST

D_MODEL = 1024
HEAD_DIM = 64
GRID_W = 64
ROPE_BASE = 10000.0
EPS = 1e-6
NEG_INF = -1e30
ATT_HEADS = 8
ATT_KV_HEADS = 2
ATT_GROUP = ATT_HEADS // ATT_KV_HEADS
ATT_WIDTH = ATT_HEADS * HEAD_DIM
KV_WIDTH = ATT_KV_HEADS * HEAD_DIM
BLOCK = 128
RET_HEADS = 4
RET_WIDTH = RET_HEADS * HEAD_DIM
RET_CHUNK = 128
CONV_WIDTH = D_MODEL - ATT_WIDTH - RET_WIDTH
CONV_KERNEL = 31
CONV_PAD = CONV_KERNEL // 2
N_EXPERTS = 16
CAPACITY_FACTOR = 2
COL_AQ = 0
COL_AK = COL_AQ + ATT_WIDTH
COL_AV = COL_AK + KV_WIDTH
COL_RQ = COL_AV + KV_WIDTH
COL_RK = COL_RQ + RET_WIDTH
COL_RV = COL_RK + RET_WIDTH
COL_RG = COL_RV + RET_WIDTH
COL_CV = COL_RG + RET_WIDTH
IN_COLS = COL_CV + 2 * CONV_WIDTH

LANES = 128
SUBLANES = 8
VMEM_LIMIT = 56 * 1024 * 1024
MOD_ROWS = 24


def _cparams(*sem):
    return pltpu.CompilerParams(dimension_semantics=sem, vmem_limit_bytes=VMEM_LIMIT)


def _sigmoid(x):
    return 1.0 / (1.0 + jnp.exp(-x))


def _silu(x):
    return x * _sigmoid(x)


def _mod_kernel(c_ref, w_ref, b_ref, o_ref):
    s = _silu(c_ref[...])
    o_ref[0] = jnp.dot(s, w_ref[0], precision=HIGHEST, preferred_element_type=F32) + b_ref[0]


def _modulation(cvec, w_mod, b_mod):
    depth, d, n = w_mod.shape
    tn = 1536
    return pl.pallas_call(
        _mod_kernel,
        out_shape=jax.ShapeDtypeStruct((depth, MOD_ROWS, n), F32),
        grid=(depth, n // tn),
        in_specs=[pl.BlockSpec((MOD_ROWS, d), lambda l, j: (0, 0)),
                  pl.BlockSpec((1, d, tn), lambda l, j: (l, 0, j)),
                  pl.BlockSpec((1, 1, tn), lambda l, j: (l, 0, j))],
        out_specs=pl.BlockSpec((1, MOD_ROWS, tn), lambda l, j: (l, 0, j)),
        compiler_params=_cparams("arbitrary", "arbitrary"),
        name="modulation",
    )(cvec, w_mod, b_mod.reshape(depth, 1, n))


def _pair_rms(u, gain):
    lane = lax.broadcasted_iota(jnp.int32, u.shape, 1)
    lo = lane < HEAD_DIM
    sq = u * u
    s_all = jnp.sum(sq, axis=1, keepdims=True)
    s_lo = jnp.sum(jnp.where(lo, sq, 0.0), axis=1, keepdims=True)
    ms = jnp.where(lo, s_lo, s_all - s_lo) * (1.0 / HEAD_DIM)
    return u * lax.rsqrt(ms + EPS) * gain


def _lane_rotary(u, cos, sins, half):
    lane = lax.broadcasted_iota(jnp.int32, u.shape, 1)
    ahead = pltpu.roll(u, LANES - half, 1)
    behind = pltpu.roll(u, half, 1)
    partner = jnp.where((lane & (2 * half - 1)) < half, ahead, behind)
    return u * cos + partner * sins


def _proj_kernel(x_ref, g_ref, sh_ref, sc_ref, w_ref, wkt_ref, cosa_ref, sina_ref, cosr_ref, sinr_ref,
                 cosk_ref, sink_ref, qg_ref, kg_ref,
                 q_ref, k_ref, v_ref, rq_ref, rkt_ref, rv_ref, rg_ref, hc_ref):
    x = x_ref[0]
    ms = jnp.mean(x * x, axis=-1, keepdims=True)
    h = x * lax.rsqrt(ms + EPS) * g_ref[...]
    h = h * (1.0 + sc_ref[0]) + sh_ref[0]
    hb = h.astype(BF16)

    def proj(c0, width):
        return jnp.dot(hb, w_ref[:, c0:c0 + width], preferred_element_type=F32)

    cosa, sina = cosa_ref[...], sina_ref[...]
    u = proj(COL_AQ, ATT_WIDTH)
    for j in range(ATT_WIDTH // LANES):
        t = _pair_rms(u[:, j * LANES:(j + 1) * LANES], qg_ref[...])
        t = _lane_rotary(t, cosa, sina, HEAD_DIM // 4)
        q_ref[0, :, j * LANES:(j + 1) * LANES] = (t * HEAD_DIM ** -0.5).astype(BF16)
    u = proj(COL_AK, 2 * KV_WIDTH)
    t = _pair_rms(u[:, :KV_WIDTH], kg_ref[...])
    k_ref[0] = _lane_rotary(t, cosa, sina, HEAD_DIM // 4).astype(BF16)
    v_ref[0] = u[:, KV_WIDTH:].astype(BF16)
    cosr, sinr = cosr_ref[...], sinr_ref[...]
    u = proj(COL_RQ, RET_WIDTH)
    for j in range(RET_WIDTH // LANES):
        t = _lane_rotary(u[:, j * LANES:(j + 1) * LANES], cosr, sinr, HEAD_DIM // 2)
        rq_ref[0, :, j * LANES:(j + 1) * LANES] = t.astype(BF16)
    rv_ref[0] = proj(COL_RV, RET_WIDTH).astype(BF16)
    rg_ref[0] = proj(COL_RG, RET_WIDTH)
    ut = lax.dot_general(wkt_ref[...], hb, (((1,), (1,)), ((), ())), preferred_element_type=F32)
    cosk, sink = cosk_ref[...], sink_ref[...]
    half = HEAD_DIM // 2
    for hd in range(RET_HEADS):
        x1 = ut[hd * HEAD_DIM:hd * HEAD_DIM + half]
        x2 = ut[hd * HEAD_DIM + half:(hd + 1) * HEAD_DIM]
        rkt_ref[0, hd * HEAD_DIM:hd * HEAD_DIM + half, :] = ((x1 * cosk - x2 * sink) * HEAD_DIM ** -0.5).astype(BF16)
        rkt_ref[0, hd * HEAD_DIM + half:(hd + 1) * HEAD_DIM, :] = ((x1 * sink + x2 * cosk) * HEAD_DIM ** -0.5).astype(BF16)
    u = proj(COL_CV, 2 * CONV_WIDTH)
    hc_ref[0] = u[:, :CONV_WIDTH] * _sigmoid(u[:, CONV_WIDTH:])


def _project(x, g, sh, sc, w_bf, wkt_bf, tabs, qg, kg):
    b, l, d = x.shape
    tm = min(512, l)
    cosa, sina, cosr, sinr, cosk, sink = tabs
    mod_map = (lambda i, j: (i, 0, 0)) if sh.shape[0] == b else (lambda i, j: (0, 0, 0))
    row = lambda i, j: (i, j, 0)
    tab = lambda i, j: (j, 0)
    const = lambda i, j: (0, 0)
    out_shape = (
        jax.ShapeDtypeStruct((b, l, ATT_WIDTH), BF16),
        jax.ShapeDtypeStruct((b, l, KV_WIDTH), BF16),
        jax.ShapeDtypeStruct((b, l, KV_WIDTH), BF16),
        jax.ShapeDtypeStruct((b, l, RET_WIDTH), BF16),
        jax.ShapeDtypeStruct((b, RET_WIDTH, l), BF16),
        jax.ShapeDtypeStruct((b, l, RET_WIDTH), BF16),
        jax.ShapeDtypeStruct((b, l, RET_WIDTH), F32),
        jax.ShapeDtypeStruct((b, l, CONV_WIDTH), F32),
    )
    out_specs = (
        pl.BlockSpec((1, tm, ATT_WIDTH), row),
        pl.BlockSpec((1, tm, KV_WIDTH), row),
        pl.BlockSpec((1, tm, KV_WIDTH), row),
        pl.BlockSpec((1, tm, RET_WIDTH), row),
        pl.BlockSpec((1, RET_WIDTH, tm), lambda i, j: (i, 0, j)),
        pl.BlockSpec((1, tm, RET_WIDTH), row),
        pl.BlockSpec((1, tm, RET_WIDTH), row),
        pl.BlockSpec((1, tm, CONV_WIDTH), row),
    )
    in_specs = [
        pl.BlockSpec((1, tm, d), row),
        pl.BlockSpec((1, d), const),
        pl.BlockSpec((1, 1, d), mod_map),
        pl.BlockSpec((1, 1, d), mod_map),
        pl.BlockSpec((d, IN_COLS), const),
        pl.BlockSpec((RET_WIDTH, d), const),
        pl.BlockSpec((tm, LANES), tab),
        pl.BlockSpec((tm, LANES), tab),
        pl.BlockSpec((tm, LANES), tab),
        pl.BlockSpec((tm, LANES), tab),
        pl.BlockSpec((HEAD_DIM // 2, tm), lambda i, j: (0, j)),
        pl.BlockSpec((HEAD_DIM // 2, tm), lambda i, j: (0, j)),
        pl.BlockSpec((1, LANES), const),
        pl.BlockSpec((1, LANES), const),
    ]
    return pl.pallas_call(
        _proj_kernel, out_shape=out_shape, grid=(b, l // tm), in_specs=in_specs, out_specs=out_specs,
        compiler_params=_cparams("arbitrary", "arbitrary"), name="in_proj",
    )(x, g, sh, sc, w_bf, wkt_bf, cosa, sina, cosr, sinr, cosk, sink, qg, kg)


def _softmax_pv(qs, pieces, sink_col):
    scores = []
    for k, _, mask in pieces:
        s = lax.dot_general(qs, k, (((1,), (1,)), ((), ())), preferred_element_type=F32)
        if mask is not None:
            s = jnp.where(mask, s, NEG_INF)
        scores.append(s)
    m_vec = None
    for s in scores:
        for j in range(s.shape[1] // LANES):
            chunk = s[:, j * LANES:(j + 1) * LANES]
            m_vec = chunk if m_vec is None else jnp.maximum(m_vec, chunk)
    m = jnp.maximum(sink_col, jnp.max(m_vec, axis=1, keepdims=True))
    p_vec = None
    acc = None
    for s, (_, v, _) in zip(scores, pieces):
        p = jnp.exp(s - m)
        for j in range(p.shape[1] // LANES):
            chunk = p[:, j * LANES:(j + 1) * LANES]
            p_vec = chunk if p_vec is None else p_vec + chunk
        o = jnp.dot(p.astype(BF16), v, preferred_element_type=F32)
        acc = o if acc is None else acc + o
    den = jnp.exp(sink_col - m) + jnp.sum(p_vec, axis=1, keepdims=True)
    return acc / den


def _sink_column(sink_ref, kh, rows_per_head):
    row = lax.broadcasted_iota(jnp.int32, (ATT_GROUP * rows_per_head, 1), 0)
    col = jnp.full(row.shape, sink_ref[kh * ATT_GROUP + ATT_GROUP - 1], F32)
    for g in range(ATT_GROUP - 2, -1, -1):
        col = jnp.where(row < (g + 1) * rows_per_head, sink_ref[kh * ATT_GROUP + g], col)
    return col


def _stack_heads(q_ref, kh):
    return jnp.concatenate(
        [q_ref[0, :, (kh * ATT_GROUP + g) * HEAD_DIM:(kh * ATT_GROUP + g + 1) * HEAD_DIM] for g in range(ATT_GROUP)],
        axis=0)


def _unstack_heads(o_ref, kh, out, rows_per_head):
    for g in range(ATT_GROUP):
        hd = kh * ATT_GROUP + g
        o_ref[0, :, hd * HEAD_DIM:(hd + 1) * HEAD_DIM] = out[g * rows_per_head:(g + 1) * rows_per_head].astype(BF16)


def _att_lat_kernel(sink_ref, q_ref, kp_ref, kc_ref, kn_ref, vp_ref, vc_ref, vn_ref, kx_ref, vx_ref, o_ref):
    n = pl.program_id(1)
    nb = pl.num_programs(1)
    shape = (ATT_GROUP * BLOCK, BLOCK)
    a = lax.broadcasted_iota(jnp.int32, shape, 0) & (BLOCK - 1)
    w = lax.broadcasted_iota(jnp.int32, shape, 1)
    far = 4 * BLOCK
    mask_p = w >= a + jnp.where(n > 0, 0, far)
    mask_n = w <= a - jnp.where(n < nb - 1, 0, far)
    for kh in range(ATT_KV_HEADS):
        sl = slice(kh * HEAD_DIM, (kh + 1) * HEAD_DIM)
        pieces = [(kp_ref[0, :, sl], vp_ref[0, :, sl], mask_p),
                  (kc_ref[0, :, sl], vc_ref[0, :, sl], None),
                  (kn_ref[0, :, sl], vn_ref[0, :, sl], mask_n),
                  (kx_ref[0, :, sl], vx_ref[0, :, sl], None)]
        out = _softmax_pv(_stack_heads(q_ref, kh), pieces, _sink_column(sink_ref, kh, BLOCK))
        _unstack_heads(o_ref, kh, out, BLOCK)


def _attention_latent(q, k, v, kx, vx, sink):
    b, s, _ = q.shape
    c = kx.shape[1]
    nb = s // BLOCK
    cur = lambda i, n: (i, n, 0)
    prv = lambda i, n: (i, jnp.maximum(n - 1, 0), 0)
    nxt = lambda i, n: (i, jnp.minimum(n + 1, nb - 1), 0)
    ctx = lambda i, n: (i, 0, 0)
    kv = lambda m: pl.BlockSpec((1, BLOCK, KV_WIDTH), m)
    return pl.pallas_call(
        _att_lat_kernel,
        out_shape=jax.ShapeDtypeStruct((b, s, ATT_WIDTH), BF16),
        grid=(b, nb),
        in_specs=[pl.BlockSpec(memory_space=pltpu.SMEM),
                  pl.BlockSpec((1, BLOCK, ATT_WIDTH), cur),
                  kv(prv), kv(cur), kv(nxt), kv(prv), kv(cur), kv(nxt),
                  pl.BlockSpec((1, c, KV_WIDTH), ctx), pl.BlockSpec((1, c, KV_WIDTH), ctx)],
        out_specs=pl.BlockSpec((1, BLOCK, ATT_WIDTH), cur),
        compiler_params=_cparams("arbitrary", "arbitrary"), name="att_latent",
    )(sink, q, k, k, k, v, v, v, kx, vx)


def _att_ctx_kernel(sink_ref, q_ref, k_ref, v_ref, o_ref):
    c = q_ref.shape[1]
    for kh in range(ATT_KV_HEADS):
        sl = slice(kh * HEAD_DIM, (kh + 1) * HEAD_DIM)
        out = _softmax_pv(_stack_heads(q_ref, kh), [(k_ref[0, :, sl], v_ref[0, :, sl], None)],
                          _sink_column(sink_ref, kh, c))
        _unstack_heads(o_ref, kh, out, c)


def _attention_context(q, k, v, sink):
    b, c, _ = q.shape
    blk = lambda w: pl.BlockSpec((1, c, w), lambda i: (i, 0, 0))
    return pl.pallas_call(
        _att_ctx_kernel,
        out_shape=jax.ShapeDtypeStruct((b, c, ATT_WIDTH), BF16),
        grid=(b,),
        in_specs=[pl.BlockSpec(memory_space=pltpu.SMEM), blk(ATT_WIDTH), blk(KV_WIDTH), blk(KV_WIDTH)],
        out_specs=blk(ATT_WIDTH),
        compiler_params=_cparams("arbitrary"), name="att_context",
    )(sink, q, k, v)


def _ret_scan_kernel(lg_ref, s0_ref, qf_ref, ktf_ref, vf_ref, qb_ref, ktb_ref, vb_ref,
                     of_ref, ob_ref, sfin_ref, st_s):
    c = pl.program_id(1)
    t = RET_CHUNK

    @pl.when(c == 0)
    def _():
        st_s[...] = s0_ref[0]

    i_col = lax.broadcasted_iota(jnp.int32, (t, HEAD_DIM), 0).astype(F32)
    j_row = lax.broadcasted_iota(jnp.int32, (HEAD_DIM, t), 1).astype(F32)
    ii = lax.broadcasted_iota(jnp.int32, (t, t), 0)
    jj = lax.broadcasted_iota(jnp.int32, (t, t), 1)
    dist = (ii - jj).astype(F32)
    for hd in range(RET_HEADS):
        lgf = lg_ref[0, hd]
        lgb = lg_ref[1, hd]
        sl = slice(hd * HEAD_DIM, (hd + 1) * HEAD_DIM)
        q = qf_ref[0, :, sl]
        kt = ktf_ref[0, sl, :]
        v = vf_ref[0, :, sl]
        dec = jnp.where(ii >= jj, jnp.exp(lgf * jnp.maximum(dist, 0.0)), jnp.exp(lgb * jnp.maximum(-dist, 0.0)))
        s = jnp.dot(q, kt, preferred_element_type=F32) * dec
        o = jnp.dot(s.astype(BF16), v, preferred_element_type=F32)
        st = st_s[0, hd]
        qd = (q.astype(F32) * jnp.exp(lgf * (i_col + 1.0))).astype(BF16)
        of_ref[0, :, sl] = o + jnp.dot(qd, st.astype(BF16), preferred_element_type=F32)
        kd = (kt.astype(F32) * jnp.exp(lgf * (t - 1.0 - j_row))).astype(BF16)
        st_s[0, hd] = jnp.exp(lgf * t) * st + jnp.dot(kd, v, preferred_element_type=F32)
        q = qb_ref[0, :, sl]
        kt = ktb_ref[0, sl, :]
        v = vb_ref[0, :, sl]
        st = st_s[1, hd]
        qd = (q.astype(F32) * jnp.exp(lgb * (t - i_col))).astype(BF16)
        ob_ref[0, :, sl] = jnp.dot(qd, st.astype(BF16), preferred_element_type=F32)
        kd = (kt.astype(F32) * jnp.exp(lgb * j_row)).astype(BF16)
        st_s[1, hd] = jnp.exp(lgb * t) * st + jnp.dot(kd, v, preferred_element_type=F32)

    @pl.when(c == pl.num_programs(1) - 1)
    def _():
        sfin_ref[0] = st_s[...]


def _retention_scan(q, kt, v, s0, log_g):
    b, l, _ = q.shape
    n = l // RET_CHUNK
    fwd = lambda i, c: (i, c, 0)
    bwd = lambda i, c: (i, n - 1 - c, 0)
    fwd_t = lambda i, c: (i, 0, c)
    bwd_t = lambda i, c: (i, 0, n - 1 - c)
    st_map = lambda i, c: (i, 0, 0, 0, 0)
    tok = lambda m: pl.BlockSpec((1, RET_CHUNK, RET_WIDTH), m)
    feat = lambda m: pl.BlockSpec((1, RET_WIDTH, RET_CHUNK), m)
    st_blk = pl.BlockSpec((1, 2, RET_HEADS, HEAD_DIM, HEAD_DIM), st_map)
    return pl.pallas_call(
        _ret_scan_kernel,
        out_shape=(jax.ShapeDtypeStruct((b, l, RET_WIDTH), F32),
                   jax.ShapeDtypeStruct((b, l, RET_WIDTH), F32),
                   jax.ShapeDtypeStruct((b, 2, RET_HEADS, HEAD_DIM, HEAD_DIM), F32)),
        grid=(b, n),
        in_specs=[pl.BlockSpec(memory_space=pltpu.SMEM), st_blk,
                  tok(fwd), feat(fwd_t), tok(fwd), tok(bwd), feat(bwd_t), tok(bwd)],
        out_specs=(tok(fwd), tok(bwd), st_blk),
        scratch_shapes=[pltpu.VMEM((2, RET_HEADS, HEAD_DIM, HEAD_DIM), F32)],
        compiler_params=_cparams("arbitrary", "arbitrary"), name="retention_scan",
    )(log_g, s0, q, kt, v, q, kt, v)


def _ret_out_kernel(of_ref, ob_ref, gate_ref, g_ref, o_ref):
    for j in range(RET_WIDTH // LANES):
        sl = slice(j * LANES, (j + 1) * LANES)
        o = of_ref[0, :, sl] + ob_ref[0, :, sl]
        lane = lax.broadcasted_iota(jnp.int32, o.shape, 1)
        lo = lane < HEAD_DIM

        def pair_mean(z):
            s_all = jnp.sum(z, axis=1, keepdims=True)
            s_lo = jnp.sum(jnp.where(lo, z, 0.0), axis=1, keepdims=True)
            return jnp.where(lo, s_lo, s_all - s_lo) * (1.0 / HEAD_DIM)

        cen = o - pair_mean(o)
        var = pair_mean(cen * cen)
        y = cen * lax.rsqrt(var + EPS) * g_ref[:, sl]
        o_ref[0, :, sl] = (y * _silu(gate_ref[0, :, sl])).astype(BF16)


def _retention_out(o_f, o_b, gate, g):
    b, l, w = o_f.shape
    tm = min(512, l)
    blk = pl.BlockSpec((1, tm, w), lambda i, j: (i, j, 0))
    return pl.pallas_call(
        _ret_out_kernel,
        out_shape=jax.ShapeDtypeStruct((b, l, w), BF16),
        grid=(b, l // tm),
        in_specs=[blk, blk, blk, pl.BlockSpec((1, w), lambda i, j: (0, 0))],
        out_specs=blk,
        compiler_params=_cparams("arbitrary", "arbitrary"), name="retention_out",
    )(o_f, o_b, gate, g)


CONV_HALO = 16
CONV_ROWS = 128


def _conv_kernel(h_ref, w_ref, b_ref, g_ref, beta_ref, o_ref, pad_s):
    l = h_ref.shape[1]
    zeros = jnp.zeros((CONV_HALO, CONV_WIDTH), F32)
    pad_s[0:CONV_HALO, :] = zeros
    pad_s[CONV_HALO + l:2 * CONV_HALO + l, :] = zeros
    pad_s[CONV_HALO:CONV_HALO + l, :] = h_ref[0]
    for r0 in range(0, l, CONV_ROWS):
        acc = jnp.zeros((CONV_ROWS, CONV_WIDTH), F32) + b_ref[...]
        for j in range(CONV_KERNEL):
            start = r0 + j + CONV_HALO - CONV_PAD
            acc = acc + pad_s[start:start + CONV_ROWS, :] * w_ref[j:j + 1, :]
        mu = jnp.mean(acc, axis=1, keepdims=True)
        cen = acc - mu
        var = jnp.mean(cen * cen, axis=1, keepdims=True)
        y = cen * lax.rsqrt(var + EPS) * g_ref[...] + beta_ref[...]
        o_ref[0, r0:r0 + CONV_ROWS, :] = _silu(y).astype(BF16)


def _conformer_conv(hc, w, bias, g, beta):
    b, l, cw = hc.shape
    blk = pl.BlockSpec((1, l, cw), lambda i: (i, 0, 0))
    vec = pl.BlockSpec((1, cw), lambda i: (0, 0))
    return pl.pallas_call(
        _conv_kernel,
        out_shape=jax.ShapeDtypeStruct((b, l, cw), BF16),
        grid=(b,),
        in_specs=[blk, pl.BlockSpec((CONV_KERNEL, cw), lambda i: (0, 0)), vec, vec, vec],
        out_specs=blk,
        scratch_shapes=[pltpu.VMEM((l + 2 * CONV_HALO, cw), F32)],
        compiler_params=_cparams("arbitrary"), name="conformer_conv",
    )(hc, w, bias, g, beta)


def _out_kernel(att_ref, ret_ref, conv_ref, w_ref, x_ref, gt_ref, g_ref, sh_ref, sc_ref, wr_ref,
                xmid_ref, hf_ref, aff_ref):
    mix = jnp.dot(att_ref[0], w_ref[0:ATT_WIDTH, :], preferred_element_type=F32)
    mix = mix + jnp.dot(ret_ref[0], w_ref[ATT_WIDTH:ATT_WIDTH + RET_WIDTH, :], preferred_element_type=F32)
    mix = mix + jnp.dot(conv_ref[0], w_ref[ATT_WIDTH + RET_WIDTH:, :], preferred_element_type=F32)
    xm = x_ref[0] + gt_ref[0] * mix
    xmid_ref[0] = xm
    ms = jnp.mean(xm * xm, axis=-1, keepdims=True)
    h = xm * lax.rsqrt(ms + EPS) * g_ref[...]
    h = h * (1.0 + sc_ref[0]) + sh_ref[0]
    hf_ref[0] = h.astype(BF16)
    lt = lax.dot_general(wr_ref[...], h, (((1,), (1,)), ((), ())), precision=HIGHEST, preferred_element_type=F32)
    e = jnp.exp(lt - jnp.max(lt, axis=0, keepdims=True))
    aff_ref[0] = e / jnp.sum(e, axis=0, keepdims=True)


def _out_project(att, ret, conv, w_bf, x, gt, g, sh, sc, wr_t):
    b, l, d = x.shape
    tm = min(512, l)
    mod_map = (lambda i, j: (i, 0, 0)) if gt.shape[0] == b else (lambda i, j: (0, 0, 0))
    row = lambda i, j: (i, j, 0)
    const = lambda i, j: (0, 0)
    mod = pl.BlockSpec((1, 1, d), mod_map)
    return pl.pallas_call(
        _out_kernel,
        out_shape=(jax.ShapeDtypeStruct((b, l, d), F32),
                   jax.ShapeDtypeStruct((b, l, d), BF16),
                   jax.ShapeDtypeStruct((b, N_EXPERTS, l), F32)),
        grid=(b, l // tm),
        in_specs=[pl.BlockSpec((1, tm, ATT_WIDTH), row), pl.BlockSpec((1, tm, RET_WIDTH), row),
                  pl.BlockSpec((1, tm, CONV_WIDTH), row), pl.BlockSpec((d, d), const),
                  pl.BlockSpec((1, tm, d), row), mod, pl.BlockSpec((1, d), const), mod, mod,
                  pl.BlockSpec((N_EXPERTS, d), const)],
        out_specs=(pl.BlockSpec((1, tm, d), row), pl.BlockSpec((1, tm, d), row),
                   pl.BlockSpec((1, N_EXPERTS, tm), lambda i, j: (i, 0, j))),
        compiler_params=_cparams("arbitrary", "arbitrary"), name="out_proj",
    )(att, ret, conv, w_bf, x, gt, g, sh, sc, wr_t)


def _prefix_sum_lanes(x):
    l = x.shape[1]
    lane = lax.broadcasted_iota(jnp.int32, x.shape, 1)
    k = 1
    while k < l:
        x = x + jnp.where(lane >= k, pltpu.roll(x, k, 1), 0.0)
        k *= 2
    return x


def _select_kernel(aff_ref, pos_ref, post_ref, *, cap):
    aff = aff_ref[0]
    thr = jnp.zeros((aff.shape[0], 1), jnp.int32)
    for bit in range(30, -1, -1):
        cand = thr | (1 << bit)
        cnt = jnp.sum(jnp.where(aff >= pltpu.bitcast(cand, F32), 1.0, 0.0), axis=1, keepdims=True)
        thr = jnp.where(cnt >= cap, cand, thr)
    above = aff >= pltpu.bitcast(thr + 1, F32)
    tie = (aff >= pltpu.bitcast(thr, F32)) & jnp.logical_not(above)
    n_above = jnp.sum(jnp.where(above, 1.0, 0.0), axis=1, keepdims=True)
    tie_f = jnp.where(tie, 1.0, 0.0)
    tie_rank = _prefix_sum_lanes(tie_f) - tie_f
    sel = above | (tie & (tie_rank < cap - n_above))
    sel_f = jnp.where(sel, 1.0, 0.0)
    pos = jnp.where(sel, _prefix_sum_lanes(sel_f) - sel_f, -1.0)
    pos_ref[0] = pos
    padded = jnp.concatenate([pos, jnp.full((LANES - pos.shape[0], pos.shape[1]), -1.0, F32)], axis=0)
    post_ref[0] = padded.T


def _select(aff_t, cap):
    b, e, l = aff_t.shape
    return pl.pallas_call(
        functools.partial(_select_kernel, cap=cap),
        out_shape=(jax.ShapeDtypeStruct((b, e, l), F32), jax.ShapeDtypeStruct((b, l, LANES), F32)),
        grid=(b,),
        in_specs=[pl.BlockSpec((1, e, l), lambda i: (i, 0, 0))],
        out_specs=(pl.BlockSpec((1, e, l), lambda i: (i, 0, 0)), pl.BlockSpec((1, l, LANES), lambda i: (i, 0, 0))),
        compiler_params=_cparams("arbitrary"), name="expert_select",
    )(aff_t)


def _gather_kernel(h_ref, pos_ref, aff_ref, x_ref, g_ref, *, cap):
    l = h_ref.shape[1]
    ge = pos_ref.shape[1]
    slot = lax.broadcasted_iota(jnp.int32, (cap, l), 0).astype(F32)
    hits = [pos_ref[0, el] == slot for el in range(ge)]
    onehot = jnp.concatenate([jnp.where(hit, 1.0, 0.0).astype(BF16) for hit in hits], axis=0)
    x = jnp.dot(onehot, h_ref[0], preferred_element_type=F32).astype(BF16)
    for el, hit in enumerate(hits):
        x_ref[el] = x[el * cap:(el + 1) * cap]
        g_ref[el] = jnp.sum(jnp.where(hit, aff_ref[0, el], 0.0), axis=1, keepdims=True)


GATHER_ROWS = 512


def _gather(hf, pos, aff_t, cap):
    b, l, d = hf.shape
    e = pos.shape[1]
    ge = max(1, min(e, GATHER_ROWS // cap))
    row = pl.BlockSpec((1, ge, 1, l), lambda i, j: (i, j, 0, 0))
    return pl.pallas_call(
        functools.partial(_gather_kernel, cap=cap),
        out_shape=(jax.ShapeDtypeStruct((e, b * cap, d), BF16), jax.ShapeDtypeStruct((e, b * cap, 1), F32)),
        grid=(b, e // ge),
        in_specs=[pl.BlockSpec((1, l, d), lambda i, j: (i, 0, 0)), row, row],
        out_specs=(pl.BlockSpec((ge, cap, d), lambda i, j: (j, i, 0)), pl.BlockSpec((ge, cap, 1), lambda i, j: (j, i, 0))),
        compiler_params=_cparams("arbitrary", "arbitrary"), name="expert_gather",
    )(hf, pos.reshape(b, e, 1, l), aff_t.reshape(b, e, 1, l))


FFN_ROWS = 512
FFN_M_TILES = 2
FFN_F_TILE = 256


def _ffn_kernel(*refs, n_sets):
    wg_ref, wu_ref, wd_ref = refs[:3]
    x_refs = refs[3:3 + n_sets]
    g_refs = refs[3 + n_sets:3 + 2 * n_sets]
    y_refs = refs[3 + 2 * n_sets:3 + 3 * n_sets]
    acc_refs = refs[3 + 3 * n_sets:]
    f = pl.program_id(2)
    wg = wg_ref[0].astype(BF16)
    wu = wu_ref[0].astype(BF16)
    wd = wd_ref[0].astype(BF16)

    @pl.when(f == 0)
    def _():
        for acc_s in acc_refs:
            acc_s[...] = jnp.zeros_like(acc_s)

    for x_ref, acc_s in zip(x_refs, acc_refs):
        tm = x_ref.shape[1]
        rc = min(FFN_ROWS, tm)
        for i in range(tm // rc):
            rows = slice(i * rc, (i + 1) * rc)
            x = x_ref[0, rows, :]
            a = jnp.dot(x, wg, preferred_element_type=F32)
            u = jnp.dot(x, wu, preferred_element_type=F32)
            hm = (_silu(a) * u).astype(BF16)
            acc_s[rows, :] += jnp.dot(hm, wd, preferred_element_type=F32)

    @pl.when(f == pl.num_programs(2) - 1)
    def _():
        for acc_s, g_ref, y_ref in zip(acc_refs, g_refs, y_refs):
            y_ref[0] = (acc_s[...] * g_ref[0]).astype(BF16)


def _expert_ffn(xes, gates, w_gate, w_up, w_down, layer):
    e, _, d = xes[0].shape
    ff = w_gate.shape[-1]
    tf = FFN_F_TILE
    tms = [xe.shape[1] // FFN_M_TILES for xe in xes]
    row = lambda i, j, f: (i, j, 0)
    w_in_blk = pl.BlockSpec((None, 1, d, tf), lambda i, j, f: (layer, i, 0, f))
    return pl.pallas_call(
        functools.partial(_ffn_kernel, n_sets=len(xes)),
        out_shape=[jax.ShapeDtypeStruct(xe.shape, BF16) for xe in xes],
        grid=(e, FFN_M_TILES, ff // tf),
        in_specs=([w_in_blk, w_in_blk, pl.BlockSpec((None, 1, tf, d), lambda i, j, f: (layer, i, f, 0))]
                  + [pl.BlockSpec((1, tm, d), row) for tm in tms]
                  + [pl.BlockSpec((1, tm, 1), row) for tm in tms]),
        out_specs=[pl.BlockSpec((1, tm, d), row) for tm in tms],
        scratch_shapes=[pltpu.VMEM((tm, d), F32) for tm in tms],
        compiler_params=_cparams("arbitrary", "arbitrary", "arbitrary"), name="expert_ffn",
    )(w_gate, w_up, w_down, *xes, *gates)


def _combine_kernel(y_ref, post_ref, x_ref, gt_ref, o_ref, *, cap):
    tt = x_ref.shape[1]
    slot = lax.broadcasted_iota(jnp.int32, (tt, cap), 1).astype(F32)
    acc = jnp.zeros((tt, x_ref.shape[2]), F32)
    for ex in range(N_EXPERTS):
        onehot = jnp.where(post_ref[0, :, ex:ex + 1] == slot, 1.0, 0.0).astype(BF16)
        acc = acc + jnp.dot(onehot, y_ref[ex], preferred_element_type=F32)
    o_ref[0] = x_ref[0] + gt_ref[0] * acc


def _combine(y, pos_t, x_mid, gt, cap):
    b, l, d = x_mid.shape
    e = y.shape[0]
    tt = min(512, l)
    mod_map = (lambda i, j: (i, 0, 0)) if gt.shape[0] == b else (lambda i, j: (0, 0, 0))
    return pl.pallas_call(
        functools.partial(_combine_kernel, cap=cap),
        out_shape=jax.ShapeDtypeStruct((b, l, d), F32),
        grid=(b, l // tt),
        in_specs=[pl.BlockSpec((e, cap, d), lambda i, j: (0, i, 0)),
                  pl.BlockSpec((1, tt, LANES), lambda i, j: (i, j, 0)),
                  pl.BlockSpec((1, tt, d), lambda i, j: (i, j, 0)),
                  pl.BlockSpec((1, 1, d), mod_map)],
        out_specs=pl.BlockSpec((1, tt, d), lambda i, j: (i, j, 0)),
        compiler_params=_cparams("arbitrary", "arbitrary"), name="expert_combine",
    )(y, pos_t, x_mid, gt)


def _rotary_tables(pos, half):
    inv = ROPE_BASE ** (-np.arange(half, dtype=np.float64) / half)
    ang = pos.astype(np.float64)[:, None] * inv[None, :]
    return np.cos(ang), np.sin(ang)


def _position_tables(seq, ctx_len):
    f = lambda a: jnp.asarray(a, F32)
    t = np.arange(seq)
    cr, sr = _rotary_tables(t // GRID_W, HEAD_DIM // 4)
    cc, sc = _rotary_tables(t % GRID_W, HEAD_DIM // 4)
    cos_a = np.tile(np.concatenate([cr, cr, cc, cc], axis=1), (1, LANES // HEAD_DIM))
    sin_a = np.tile(np.concatenate([-sr, sr, -sc, sc], axis=1), (1, LANES // HEAD_DIM))

    def ret(pos):
        c, s = _rotary_tables(pos, HEAD_DIM // 2)
        return (f(np.tile(np.concatenate([c, c], axis=1), (1, LANES // HEAD_DIM))),
                f(np.tile(np.concatenate([-s, s], axis=1), (1, LANES // HEAD_DIM))),
                f(c.T), f(s.T))

    lat = (f(cos_a), f(sin_a)) + ret(ctx_len + np.arange(seq))
    ctx = (jnp.ones((ctx_len, LANES), F32), jnp.zeros((ctx_len, LANES), F32)) + ret(np.arange(ctx_len))
    return lat, ctx


def _route(hf, aff_t):
    cap = CAPACITY_FACTOR * hf.shape[1] // N_EXPERTS
    pos, pos_t = _select(aff_t, cap)
    xe, gate = _gather(hf, pos, aff_t, cap)
    return cap, pos_t, xe, gate


def kernel(x, c, ctx, c_ctx, w_mod, b_mod, g_mix, g_ffn, w_in, q_norm_g, k_norm_g, att_sink, ret_decay_logit,
           ret_norm_g, conv_w, conv_b, conv_norm_g, conv_norm_b, w_out, w_router, w_gate, w_up, w_down):
    b, s, d = x.shape
    cl = ctx.shape[1]
    depth = w_mod.shape[0]
    tabs_l, tabs_c = _position_tables(s, cl)

    cvec = jnp.concatenate([c, c_ctx[None, :], jnp.zeros((MOD_ROWS - b - 1, d), F32)], axis=0)
    mod = _modulation(cvec, w_mod, b_mod)

    w_in_bf = w_in.astype(BF16)
    w_rk_t = jnp.swapaxes(w_in[:, :, COL_RK:COL_RK + RET_WIDTH], 1, 2).astype(BF16)
    w_out_bf = w_out.astype(BF16)
    w_router_t = jnp.swapaxes(w_router, 1, 2)
    log_g = jax.nn.log_sigmoid(ret_decay_logit.astype(F32))
    tile2 = lambda g: jnp.tile(g, LANES // HEAD_DIM)[None, :]
    zero_state = jnp.zeros((b, 2, RET_HEADS, HEAD_DIM, HEAD_DIM), F32)

    xc = ctx
    for l in range(depth):
        last = l == depth - 1
        m_l = [mod[l, :b, i * d:(i + 1) * d][:, None, :] for i in range(6)]
        m_c = [mod[l, b:b + 1, i * d:(i + 1) * d][:, None, :] for i in range(6)]
        vec = lambda a: a[l][None, :]
        qg, kg = tile2(q_norm_g[l]), tile2(k_norm_g[l])

        aq_c, ak_c, av_c, rq_c, rkt_c, rv_c, rg_c, hc_c = _project(
            xc, vec(g_mix), m_c[0], m_c[1], w_in_bf[l], w_rk_t[l], tabs_c, qg, kg)
        aq_l, ak_l, av_l, rq_l, rkt_l, rv_l, rg_l, hc_l = _project(
            x, vec(g_mix), m_l[0], m_l[1], w_in_bf[l], w_rk_t[l], tabs_l, qg, kg)

        att_l = _attention_latent(aq_l, ak_l, av_l, ak_c, av_c, att_sink[l])
        of_c, ob_c, s_ctx = _retention_scan(rq_c, rkt_c, rv_c, zero_state, log_g[l])
        of_l, ob_l, _ = _retention_scan(rq_l, rkt_l, rv_l, s_ctx, log_g[l])
        ret_l = _retention_out(of_l, ob_l, rg_l, vec(ret_norm_g))
        conv_l = _conformer_conv(hc_l, conv_w[l], vec(conv_b), vec(conv_norm_g), vec(conv_norm_b))
        x_mid, hf_l, aff_l = _out_project(att_l, ret_l, conv_l, w_out_bf[l], x, m_l[2], vec(g_ffn),
                                          m_l[3], m_l[4], w_router_t[l])
        routed = [_route(hf_l, aff_l)]
        if not last:
            att_c = _attention_context(aq_c, ak_c, av_c, att_sink[l])
            ret_c = _retention_out(of_c, ob_c, rg_c, vec(ret_norm_g))
            conv_c = _conformer_conv(hc_c, conv_w[l], vec(conv_b), vec(conv_norm_g), vec(conv_norm_b))
            xc_mid, hf_c, aff_c = _out_project(att_c, ret_c, conv_c, w_out_bf[l], xc, m_c[2], vec(g_ffn),
                                               m_c[3], m_c[4], w_router_t[l])
            routed.append(_route(hf_c, aff_c))
        ys = _expert_ffn([r[2] for r in routed], [r[3] for r in routed], w_gate, w_up, w_down, l)
        x = _combine(ys[0], routed[0][1], x_mid, m_l[5], routed[0][0])
        if not last:
            xc = _combine(ys[1], routed[1][1], xc_mid, m_c[5], routed[1][0])
    return x
```

```python
import functools

import numpy as np
import jax
import jax.numpy as jnp
from jax import lax
from jax.experimental import pallas as pl
from jax.experimental.pallas import tpu as pltpu

F32 = jnp.float32
BF16 = jnp.bfloat16
HIGHEST = lax.Precision.HIGHEST

D_MODEL = 1024
HEAD_DIM = 64
GRID_W = 64
ROPE_BASE = 10000.0
EPS = 1e-6
NEG_INF = -1e30
ATT_HEADS = 8
ATT_KV_HEADS = 2
ATT_GROUP = ATT_HEADS // ATT_KV_HEADS
ATT_WIDTH = ATT_HEADS * HEAD_DIM
KV_WIDTH = ATT_KV_HEADS * HEAD_DIM
BLOCK = 128
RET_HEADS = 4
RET_WIDTH = RET_HEADS * HEAD_DIM
RET_CHUNK = 128
CONV_WIDTH = D_MODEL - ATT_WIDTH - RET_WIDTH
CONV_KERNEL = 31
CONV_PAD = CONV_KERNEL // 2
N_EXPERTS = 16
CAPACITY_FACTOR = 2
COL_AQ = 0
COL_AK = COL_AQ + ATT_WIDTH
COL_AV = COL_AK + KV_WIDTH
COL_RQ = COL_AV + KV_WIDTH
COL_RK = COL_RQ + RET_WIDTH
COL_RV = COL_RK + RET_WIDTH
COL_RG = COL_RV + RET_WIDTH
COL_CV = COL_RG + RET_WIDTH
IN_COLS = COL_CV + 2 * CONV_WIDTH

LANES = 128
SUBLANES = 8
VMEM_LIMIT = 56 * 1024 * 1024
MOD_ROWS = 24


def _cparams(*sem):
    return pltpu.CompilerParams(dimension_semantics=sem, vmem_limit_bytes=VMEM_LIMIT)


def _sigmoid(x):
    return 1.0 / (1.0 + jnp.exp(-x))


def _silu(x):
    return x * _sigmoid(x)


def _mod_kernel(c_ref, w_ref, b_ref, o_ref):
    s = _silu(c_ref[...])
    o_ref[0] = jnp.dot(s, w_ref[0], precision=HIGHEST, preferred_element_type=F32) + b_ref[0]


def _modulation(cvec, w_mod, b_mod):
    depth, d, n = w_mod.shape
    tn = 1536
    return pl.pallas_call(
        _mod_kernel,
        out_shape=jax.ShapeDtypeStruct((depth, MOD_ROWS, n), F32),
        grid=(depth, n // tn),
        in_specs=[pl.BlockSpec((MOD_ROWS, d), lambda l, j: (0, 0)),
                  pl.BlockSpec((1, d, tn), lambda l, j: (l, 0, j)),
                  pl.BlockSpec((1, 1, tn), lambda l, j: (l, 0, j))],
        out_specs=pl.BlockSpec((1, MOD_ROWS, tn), lambda l, j: (l, 0, j)),
        compiler_params=_cparams("arbitrary", "arbitrary"),
        name="modulation",
    )(cvec, w_mod, b_mod.reshape(depth, 1, n))


def _pair_rms(u, gain):
    lane = lax.broadcasted_iota(jnp.int32, u.shape, 1)
    lo = lane < HEAD_DIM
    sq = u * u
    s_all = jnp.sum(sq, axis=1, keepdims=True)
    s_lo = jnp.sum(jnp.where(lo, sq, 0.0), axis=1, keepdims=True)
    ms = jnp.where(lo, s_lo, s_all - s_lo) * (1.0 / HEAD_DIM)
    return u * lax.rsqrt(ms + EPS) * gain


def _lane_rotary(u, cos, sins, half):
    lane = lax.broadcasted_iota(jnp.int32, u.shape, 1)
    ahead = pltpu.roll(u, LANES - half, 1)
    behind = pltpu.roll(u, half, 1)
    partner = jnp.where((lane & (2 * half - 1)) < half, ahead, behind)
    return u * cos + partner * sins


def _proj_kernel(x_ref, g_ref, sh_ref, sc_ref, w_ref, wkt_ref, cosa_ref, sina_ref, cosr_ref, sinr_ref,
                 cosk_ref, sink_ref, qg_ref, kg_ref,
                 q_ref, k_ref, v_ref, rq_ref, rkt_ref, rv_ref, rg_ref, hc_ref):
    x = x_ref[0]
    ms = jnp.mean(x * x, axis=-1, keepdims=True)
    h = x * lax.rsqrt(ms + EPS) * g_ref[...]
    h = h * (1.0 + sc_ref[0]) + sh_ref[0]
    hb = h.astype(BF16)

    def proj(c0, width):
        return jnp.dot(hb, w_ref[:, c0:c0 + width], preferred_element_type=F32)

    cosa, sina = cosa_ref[...], sina_ref[...]
    u = proj(COL_AQ, ATT_WIDTH)
    for j in range(ATT_WIDTH // LANES):
        t = _pair_rms(u[:, j * LANES:(j + 1) * LANES], qg_ref[...])
        t = _lane_rotary(t, cosa, sina, HEAD_DIM // 4)
        q_ref[0, :, j * LANES:(j + 1) * LANES] = (t * HEAD_DIM ** -0.5).astype(BF16)
    u = proj(COL_AK, 2 * KV_WIDTH)
    t = _pair_rms(u[:, :KV_WIDTH], kg_ref[...])
    k_ref[0] = _lane_rotary(t, cosa, sina, HEAD_DIM // 4).astype(BF16)
    v_ref[0] = u[:, KV_WIDTH:].astype(BF16)
    cosr, sinr = cosr_ref[...], sinr_ref[...]
    u = proj(COL_RQ, RET_WIDTH)
    for j in range(RET_WIDTH // LANES):
        t = _lane_rotary(u[:, j * LANES:(j + 1) * LANES], cosr, sinr, HEAD_DIM // 2)
        rq_ref[0, :, j * LANES:(j + 1) * LANES] = t.astype(BF16)
    rv_ref[0] = proj(COL_RV, RET_WIDTH).astype(BF16)
    rg_ref[0] = proj(COL_RG, RET_WIDTH)
    ut = lax.dot_general(wkt_ref[...], hb, (((1,), (1,)), ((), ())), preferred_element_type=F32)
    cosk, sink = cosk_ref[...], sink_ref[...]
    half = HEAD_DIM // 2
    for hd in range(RET_HEADS):
        x1 = ut[hd * HEAD_DIM:hd * HEAD_DIM + half]
        x2 = ut[hd * HEAD_DIM + half:(hd + 1) * HEAD_DIM]
        rkt_ref[0, hd * HEAD_DIM:hd * HEAD_DIM + half, :] = ((x1 * cosk - x2 * sink) * HEAD_DIM ** -0.5).astype(BF16)
        rkt_ref[0, hd * HEAD_DIM + half:(hd + 1) * HEAD_DIM, :] = ((x1 * sink + x2 * cosk) * HEAD_DIM ** -0.5).astype(BF16)
    u = proj(COL_CV, 2 * CONV_WIDTH)
    hc_ref[0] = u[:, :CONV_WIDTH] * _sigmoid(u[:, CONV_WIDTH:])


def _project(x, g, sh, sc, w_bf, wkt_bf, tabs, qg, kg):
    b, l, d = x.shape
    tm = min(512, l)
    cosa, sina, cosr, sinr, cosk, sink = tabs
    mod_map = (lambda i, j: (i, 0, 0)) if sh.shape[0] == b else (lambda i, j: (0, 0, 0))
    row = lambda i, j: (i, j, 0)
    tab = lambda i, j: (j, 0)
    const = lambda i, j: (0, 0)
    out_shape = (
        jax.ShapeDtypeStruct((b, l, ATT_WIDTH), BF16),
        jax.ShapeDtypeStruct((b, l, KV_WIDTH), BF16),
        jax.ShapeDtypeStruct((b, l, KV_WIDTH), BF16),
        jax.ShapeDtypeStruct((b, l, RET_WIDTH), BF16),
        jax.ShapeDtypeStruct((b, RET_WIDTH, l), BF16),
        jax.ShapeDtypeStruct((b, l, RET_WIDTH), BF16),
        jax.ShapeDtypeStruct((b, l, RET_WIDTH), F32),
        jax.ShapeDtypeStruct((b, l, CONV_WIDTH), F32),
    )
    out_specs = (
        pl.BlockSpec((1, tm, ATT_WIDTH), row),
        pl.BlockSpec((1, tm, KV_WIDTH), row),
        pl.BlockSpec((1, tm, KV_WIDTH), row),
        pl.BlockSpec((1, tm, RET_WIDTH), row),
        pl.BlockSpec((1, RET_WIDTH, tm), lambda i, j: (i, 0, j)),
        pl.BlockSpec((1, tm, RET_WIDTH), row),
        pl.BlockSpec((1, tm, RET_WIDTH), row),
        pl.BlockSpec((1, tm, CONV_WIDTH), row),
    )
    in_specs = [
        pl.BlockSpec((1, tm, d), row),
        pl.BlockSpec((1, d), const),
        pl.BlockSpec((1, 1, d), mod_map),
        pl.BlockSpec((1, 1, d), mod_map),
        pl.BlockSpec((d, IN_COLS), const),
        pl.BlockSpec((RET_WIDTH, d), const),
        pl.BlockSpec((tm, LANES), tab),
        pl.BlockSpec((tm, LANES), tab),
        pl.BlockSpec((tm, LANES), tab),
        pl.BlockSpec((tm, LANES), tab),
        pl.BlockSpec((HEAD_DIM // 2, tm), lambda i, j: (0, j)),
        pl.BlockSpec((HEAD_DIM // 2, tm), lambda i, j: (0, j)),
        pl.BlockSpec((1, LANES), const),
        pl.BlockSpec((1, LANES), const),
    ]
    return pl.pallas_call(
        _proj_kernel, out_shape=out_shape, grid=(b, l // tm), in_specs=in_specs, out_specs=out_specs,
        compiler_params=_cparams("arbitrary", "arbitrary"), name="in_proj",
    )(x, g, sh, sc, w_bf, wkt_bf, cosa, sina, cosr, sinr, cosk, sink, qg, kg)


def _softmax_pv(qs, pieces, sink_col):
    scores = []
    for k, _, mask in pieces:
        s = lax.dot_general(qs, k, (((1,), (1,)), ((), ())), preferred_element_type=F32)
        if mask is not None:
            s = jnp.where(mask, s, NEG_INF)
        scores.append(s)
    m_vec = None
    for s in scores:
        for j in range(s.shape[1] // LANES):
            chunk = s[:, j * LANES:(j + 1) * LANES]
            m_vec = chunk if m_vec is None else jnp.maximum(m_vec, chunk)
    m = jnp.maximum(sink_col, jnp.max(m_vec, axis=1, keepdims=True))
    p_vec = None
    acc = None
    for s, (_, v, _) in zip(scores, pieces):
        p = jnp.exp(s - m)
        for j in range(p.shape[1] // LANES):
            chunk = p[:, j * LANES:(j + 1) * LANES]
            p_vec = chunk if p_vec is None else p_vec + chunk
        o = jnp.dot(p.astype(BF16), v, preferred_element_type=F32)
        acc = o if acc is None else acc + o
    den = jnp.exp(sink_col - m) + jnp.sum(p_vec, axis=1, keepdims=True)
    return acc / den


def _sink_column(sink_ref, kh, rows_per_head):
    row = lax.broadcasted_iota(jnp.int32, (ATT_GROUP * rows_per_head, 1), 0)
    col = jnp.full(row.shape, sink_ref[kh * ATT_GROUP + ATT_GROUP - 1], F32)
    for g in range(ATT_GROUP - 2, -1, -1):
        col = jnp.where(row < (g + 1) * rows_per_head, sink_ref[kh * ATT_GROUP + g], col)
    return col


def _stack_heads(q_ref, kh):
    return jnp.concatenate(
        [q_ref[0, :, (kh * ATT_GROUP + g) * HEAD_DIM:(kh * ATT_GROUP + g + 1) * HEAD_DIM] for g in range(ATT_GROUP)],
        axis=0)


def _unstack_heads(o_ref, kh, out, rows_per_head):
    for g in range(ATT_GROUP):
        hd = kh * ATT_GROUP + g
        o_ref[0, :, hd * HEAD_DIM:(hd + 1) * HEAD_DIM] = out[g * rows_per_head:(g + 1) * rows_per_head].astype(BF16)


def _att_lat_kernel(sink_ref, q_ref, kp_ref, kc_ref, kn_ref, vp_ref, vc_ref, vn_ref, kx_ref, vx_ref, o_ref):
    n = pl.program_id(1)
    nb = pl.num_programs(1)
    shape = (ATT_GROUP * BLOCK, BLOCK)
    a = lax.broadcasted_iota(jnp.int32, shape, 0) & (BLOCK - 1)
    w = lax.broadcasted_iota(jnp.int32, shape, 1)
    far = 4 * BLOCK
    mask_p = w >= a + jnp.where(n > 0, 0, far)
    mask_n = w <= a - jnp.where(n < nb - 1, 0, far)
    for kh in range(ATT_KV_HEADS):
        sl = slice(kh * HEAD_DIM, (kh + 1) * HEAD_DIM)
        pieces = [(kp_ref[0, :, sl], vp_ref[0, :, sl], mask_p),
                  (kc_ref[0, :, sl], vc_ref[0, :, sl], None),
                  (kn_ref[0, :, sl], vn_ref[0, :, sl], mask_n),
                  (kx_ref[0, :, sl], vx_ref[0, :, sl], None)]
        out = _softmax_pv(_stack_heads(q_ref, kh), pieces, _sink_column(sink_ref, kh, BLOCK))
        _unstack_heads(o_ref, kh, out, BLOCK)


def _attention_latent(q, k, v, kx, vx, sink):
    b, s, _ = q.shape
    c = kx.shape[1]
    nb = s // BLOCK
    cur = lambda i, n: (i, n, 0)
    prv = lambda i, n: (i, jnp.maximum(n - 1, 0), 0)
    nxt = lambda i, n: (i, jnp.minimum(n + 1, nb - 1), 0)
    ctx = lambda i, n: (i, 0, 0)
    kv = lambda m: pl.BlockSpec((1, BLOCK, KV_WIDTH), m)
    return pl.pallas_call(
        _att_lat_kernel,
        out_shape=jax.ShapeDtypeStruct((b, s, ATT_WIDTH), BF16),
        grid=(b, nb),
        in_specs=[pl.BlockSpec(memory_space=pltpu.SMEM),
                  pl.BlockSpec((1, BLOCK, ATT_WIDTH), cur),
                  kv(prv), kv(cur), kv(nxt), kv(prv), kv(cur), kv(nxt),
                  pl.BlockSpec((1, c, KV_WIDTH), ctx), pl.BlockSpec((1, c, KV_WIDTH), ctx)],
        out_specs=pl.BlockSpec((1, BLOCK, ATT_WIDTH), cur),
        compiler_params=_cparams("arbitrary", "arbitrary"), name="att_latent",
    )(sink, q, k, k, k, v, v, v, kx, vx)


def _att_ctx_kernel(sink_ref, q_ref, k_ref, v_ref, o_ref):
    c = q_ref.shape[1]
    for kh in range(ATT_KV_HEADS):
        sl = slice(kh * HEAD_DIM, (kh + 1) * HEAD_DIM)
        out = _softmax_pv(_stack_heads(q_ref, kh), [(k_ref[0, :, sl], v_ref[0, :, sl], None)],
                          _sink_column(sink_ref, kh, c))
        _unstack_heads(o_ref, kh, out, c)


def _attention_context(q, k, v, sink):
    b, c, _ = q.shape
    blk = lambda w: pl.BlockSpec((1, c, w), lambda i: (i, 0, 0))
    return pl.pallas_call(
        _att_ctx_kernel,
        out_shape=jax.ShapeDtypeStruct((b, c, ATT_WIDTH), BF16),
        grid=(b,),
        in_specs=[pl.BlockSpec(memory_space=pltpu.SMEM), blk(ATT_WIDTH), blk(KV_WIDTH), blk(KV_WIDTH)],
        out_specs=blk(ATT_WIDTH),
        compiler_params=_cparams("arbitrary"), name="att_context",
    )(sink, q, k, v)


def _ret_scan_kernel(lg_ref, s0_ref, qf_ref, ktf_ref, vf_ref, qb_ref, ktb_ref, vb_ref,
                     of_ref, ob_ref, sfin_ref, st_s):
    c = pl.program_id(1)
    t = RET_CHUNK

    @pl.when(c == 0)
    def _():
        st_s[...] = s0_ref[0]

    i_col = lax.broadcasted_iota(jnp.int32, (t, HEAD_DIM), 0).astype(F32)
    j_row = lax.broadcasted_iota(jnp.int32, (HEAD_DIM, t), 1).astype(F32)
    ii = lax.broadcasted_iota(jnp.int32, (t, t), 0)
    jj = lax.broadcasted_iota(jnp.int32, (t, t), 1)
    dist = (ii - jj).astype(F32)
    for hd in range(RET_HEADS):
        lgf = lg_ref[0, hd]
        lgb = lg_ref[1, hd]
        sl = slice(hd * HEAD_DIM, (hd + 1) * HEAD_DIM)
        q = qf_ref[0, :, sl]
        kt = ktf_ref[0, sl, :]
        v = vf_ref[0, :, sl]
        dec = jnp.where(ii >= jj, jnp.exp(lgf * jnp.maximum(dist, 0.0)), jnp.exp(lgb * jnp.maximum(-dist, 0.0)))
        s = jnp.dot(q, kt, preferred_element_type=F32) * dec
        o = jnp.dot(s.astype(BF16), v, preferred_element_type=F32)
        st = st_s[0, hd]
        qd = (q.astype(F32) * jnp.exp(lgf * (i_col + 1.0))).astype(BF16)
        of_ref[0, :, sl] = o + jnp.dot(qd, st.astype(BF16), preferred_element_type=F32)
        kd = (kt.astype(F32) * jnp.exp(lgf * (t - 1.0 - j_row))).astype(BF16)
        st_s[0, hd] = jnp.exp(lgf * t) * st + jnp.dot(kd, v, preferred_element_type=F32)
        q = qb_ref[0, :, sl]
        kt = ktb_ref[0, sl, :]
        v = vb_ref[0, :, sl]
        st = st_s[1, hd]
        qd = (q.astype(F32) * jnp.exp(lgb * (t - i_col))).astype(BF16)
        ob_ref[0, :, sl] = jnp.dot(qd, st.astype(BF16), preferred_element_type=F32)
        kd = (kt.astype(F32) * jnp.exp(lgb * j_row)).astype(BF16)
        st_s[1, hd] = jnp.exp(lgb * t) * st + jnp.dot(kd, v, preferred_element_type=F32)

    @pl.when(c == pl.num_programs(1) - 1)
    def _():
        sfin_ref[0] = st_s[...]


def _retention_scan(q, kt, v, s0, log_g):
    b, l, _ = q.shape
    n = l // RET_CHUNK
    fwd = lambda i, c: (i, c, 0)
    bwd = lambda i, c: (i, n - 1 - c, 0)
    fwd_t = lambda i, c: (i, 0, c)
    bwd_t = lambda i, c: (i, 0, n - 1 - c)
    st_map = lambda i, c: (i, 0, 0, 0, 0)
    tok = lambda m: pl.BlockSpec((1, RET_CHUNK, RET_WIDTH), m)
    feat = lambda m: pl.BlockSpec((1, RET_WIDTH, RET_CHUNK), m)
    st_blk = pl.BlockSpec((1, 2, RET_HEADS, HEAD_DIM, HEAD_DIM), st_map)
    return pl.pallas_call(
        _ret_scan_kernel,
        out_shape=(jax.ShapeDtypeStruct((b, l, RET_WIDTH), F32),
                   jax.ShapeDtypeStruct((b, l, RET_WIDTH), F32),
                   jax.ShapeDtypeStruct((b, 2, RET_HEADS, HEAD_DIM, HEAD_DIM), F32)),
        grid=(b, n),
        in_specs=[pl.BlockSpec(memory_space=pltpu.SMEM), st_blk,
                  tok(fwd), feat(fwd_t), tok(fwd), tok(bwd), feat(bwd_t), tok(bwd)],
        out_specs=(tok(fwd), tok(bwd), st_blk),
        scratch_shapes=[pltpu.VMEM((2, RET_HEADS, HEAD_DIM, HEAD_DIM), F32)],
        compiler_params=_cparams("arbitrary", "arbitrary"), name="retention_scan",
    )(log_g, s0, q, kt, v, q, kt, v)


def _ret_out_kernel(of_ref, ob_ref, gate_ref, g_ref, o_ref):
    for j in range(RET_WIDTH // LANES):
        sl = slice(j * LANES, (j + 1) * LANES)
        o = of_ref[0, :, sl] + ob_ref[0, :, sl]
        lane = lax.broadcasted_iota(jnp.int32, o.shape, 1)
        lo = lane < HEAD_DIM

        def pair_mean(z):
            s_all = jnp.sum(z, axis=1, keepdims=True)
            s_lo = jnp.sum(jnp.where(lo, z, 0.0), axis=1, keepdims=True)
            return jnp.where(lo, s_lo, s_all - s_lo) * (1.0 / HEAD_DIM)

        cen = o - pair_mean(o)
        var = pair_mean(cen * cen)
        y = cen * lax.rsqrt(var + EPS) * g_ref[:, sl]
        o_ref[0, :, sl] = (y * _silu(gate_ref[0, :, sl])).astype(BF16)


def _retention_out(o_f, o_b, gate, g):
    b, l, w = o_f.shape
    tm = min(512, l)
    blk = pl.BlockSpec((1, tm, w), lambda i, j: (i, j, 0))
    return pl.pallas_call(
        _ret_out_kernel,
        out_shape=jax.ShapeDtypeStruct((b, l, w), BF16),
        grid=(b, l // tm),
        in_specs=[blk, blk, blk, pl.BlockSpec((1, w), lambda i, j: (0, 0))],
        out_specs=blk,
        compiler_params=_cparams("arbitrary", "arbitrary"), name="retention_out",
    )(o_f, o_b, gate, g)


CONV_HALO = 16
CONV_ROWS = 128
CONV_COPY_ROWS = 256


def _conv_kernel(h_ref, w_ref, b_ref, g_ref, beta_ref, o_ref, pad_s, shift_s):
    l = h_ref.shape[1]
    zeros = jnp.zeros((CONV_HALO, CONV_WIDTH), F32)
    pad_s[0:CONV_HALO, :] = zeros
    pad_s[CONV_HALO + l:2 * CONV_HALO + l, :] = zeros
    pad_s[CONV_HALO:CONV_HALO + l, :] = h_ref[0]
    ext = shift_s.shape[1]
    for r in range(1, SUBLANES):
        for c0 in range(0, ext, CONV_COPY_ROWS):
            n = min(CONV_COPY_ROWS, ext - c0)
            shift_s[r - 1, c0:c0 + n, :] = pad_s[c0 + r:c0 + r + n, :]
    for r0 in range(0, l, CONV_ROWS):
        acc = jnp.zeros((CONV_ROWS, CONV_WIDTH), F32) + b_ref[...]
        for j in range(CONV_KERNEL):
            off = j + CONV_HALO - CONV_PAD
            start = r0 + (off // SUBLANES) * SUBLANES
            if off % SUBLANES == 0:
                rows = pad_s[start:start + CONV_ROWS, :]
            else:
                rows = shift_s[off % SUBLANES - 1, start:start + CONV_ROWS, :]
            acc = acc + rows * w_ref[j:j + 1, :]
        mu = jnp.mean(acc, axis=1, keepdims=True)
        cen = acc - mu
        var = jnp.mean(cen * cen, axis=1, keepdims=True)
        y = cen * lax.rsqrt(var + EPS) * g_ref[...] + beta_ref[...]
        o_ref[0, r0:r0 + CONV_ROWS, :] = _silu(y).astype(BF16)


def _conformer_conv(hc, w, bias, g, beta):
    b, l, cw = hc.shape
    blk = pl.BlockSpec((1, l, cw), lambda i: (i, 0, 0))
    vec = pl.BlockSpec((1, cw), lambda i: (0, 0))
    return pl.pallas_call(
        _conv_kernel,
        out_shape=jax.ShapeDtypeStruct((b, l, cw), BF16),
        grid=(b,),
        in_specs=[blk, pl.BlockSpec((CONV_KERNEL, cw), lambda i: (0, 0)), vec, vec, vec],
        out_specs=blk,
        scratch_shapes=[pltpu.VMEM((l + 2 * CONV_HALO, cw), F32),
                        pltpu.VMEM((SUBLANES - 1, l + 2 * CONV_HALO - SUBLANES, cw), F32)],
        compiler_params=_cparams("arbitrary"), name="conformer_conv",
    )(hc, w, bias, g, beta)


OUT_SUB_ROWS = 256


def _out_kernel(att_ref, ret_ref, conv_ref, w_ref, x_ref, gt_ref, g_ref, sh_ref, sc_ref, wr_ref,
                xmid_ref, hf_ref, aff_ref):
    tm = x_ref.shape[1]
    sub = min(OUT_SUB_ROWS, tm)
    for i in range(tm // sub):
        rows = slice(i * sub, (i + 1) * sub)
        mix = jnp.dot(att_ref[0, rows, :], w_ref[0:ATT_WIDTH, :], preferred_element_type=F32)
        mix = mix + jnp.dot(ret_ref[0, rows, :], w_ref[ATT_WIDTH:ATT_WIDTH + RET_WIDTH, :],
                            preferred_element_type=F32)
        mix = mix + jnp.dot(conv_ref[0, rows, :], w_ref[ATT_WIDTH + RET_WIDTH:, :], preferred_element_type=F32)
        xm = x_ref[0, rows, :] + gt_ref[0] * mix
        xmid_ref[0, rows, :] = xm
        ms = jnp.mean(xm * xm, axis=-1, keepdims=True)
        h = xm * lax.rsqrt(ms + EPS) * g_ref[...]
        h = h * (1.0 + sc_ref[0]) + sh_ref[0]
        hf_ref[0, rows, :] = h.astype(BF16)
        lt = lax.dot_general(wr_ref[...], h, (((1,), (1,)), ((), ())), precision=HIGHEST,
                             preferred_element_type=F32)
        e = jnp.exp(lt - jnp.max(lt, axis=0, keepdims=True))
        aff_ref[0, :, rows] = e / jnp.sum(e, axis=0, keepdims=True)


def _out_project(att, ret, conv, w_bf, x, gt, g, sh, sc, wr_t):
    b, l, d = x.shape
    tm = min(512, l)
    mod_map = (lambda i, j: (i, 0, 0)) if gt.shape[0] == b else (lambda i, j: (0, 0, 0))
    row = lambda i, j: (i, j, 0)
    const = lambda i, j: (0, 0)
    mod = pl.BlockSpec((1, 1, d), mod_map)
    return pl.pallas_call(
        _out_kernel,
        out_shape=(jax.ShapeDtypeStruct((b, l, d), F32),
                   jax.ShapeDtypeStruct((b, l, d), BF16),
                   jax.ShapeDtypeStruct((b, N_EXPERTS, l), F32)),
        grid=(b, l // tm),
        in_specs=[pl.BlockSpec((1, tm, ATT_WIDTH), row), pl.BlockSpec((1, tm, RET_WIDTH), row),
                  pl.BlockSpec((1, tm, CONV_WIDTH), row), pl.BlockSpec((d, d), const),
                  pl.BlockSpec((1, tm, d), row), mod, pl.BlockSpec((1, d), const), mod, mod,
                  pl.BlockSpec((N_EXPERTS, d), const)],
        out_specs=(pl.BlockSpec((1, tm, d), row), pl.BlockSpec((1, tm, d), row),
                   pl.BlockSpec((1, N_EXPERTS, tm), lambda i, j: (i, 0, j))),
        compiler_params=_cparams("arbitrary", "arbitrary"), name="out_proj",
    )(att, ret, conv, w_bf, x, gt, g, sh, sc, wr_t)


def _prefix_sum_lanes(x):
    l = x.shape[1]
    lane = lax.broadcasted_iota(jnp.int32, x.shape, 1)
    k = 1
    while k < l:
        x = x + jnp.where(lane >= k, pltpu.roll(x, k, 1), 0.0)
        k *= 2
    return x


def _select_kernel(aff_ref, pos_ref, post_ref, *, cap):
    aff = aff_ref[0]
    thr = jnp.zeros((aff.shape[0], 1), jnp.int32)
    for bit in range(30, -1, -1):
        cand = thr | (1 << bit)
        cnt = jnp.sum(jnp.where(aff >= pltpu.bitcast(cand, F32), 1.0, 0.0), axis=1, keepdims=True)
        thr = jnp.where(cnt >= cap, cand, thr)
    above = aff >= pltpu.bitcast(thr + 1, F32)
    tie = (aff >= pltpu.bitcast(thr, F32)) & jnp.logical_not(above)
    n_above = jnp.sum(jnp.where(above, 1.0, 0.0), axis=1, keepdims=True)
    tie_f = jnp.where(tie, 1.0, 0.0)
    tie_rank = _prefix_sum_lanes(tie_f) - tie_f
    sel = above | (tie & (tie_rank < cap - n_above))
    sel_f = jnp.where(sel, 1.0, 0.0)
    pos = jnp.where(sel, _prefix_sum_lanes(sel_f) - sel_f, -1.0)
    pos_ref[0] = pos
    padded = jnp.concatenate([pos, jnp.full((LANES - pos.shape[0], pos.shape[1]), -1.0, F32)], axis=0)
    post_ref[0] = padded.T


def _select(aff_t, cap):
    b, e, l = aff_t.shape
    return pl.pallas_call(
        functools.partial(_select_kernel, cap=cap),
        out_shape=(jax.ShapeDtypeStruct((b, e, l), F32), jax.ShapeDtypeStruct((b, l, LANES), F32)),
        grid=(b,),
        in_specs=[pl.BlockSpec((1, e, l), lambda i: (i, 0, 0))],
        out_specs=(pl.BlockSpec((1, e, l), lambda i: (i, 0, 0)), pl.BlockSpec((1, l, LANES), lambda i: (i, 0, 0))),
        compiler_params=_cparams("arbitrary"), name="expert_select",
    )(aff_t)


def _gather_kernel(h_ref, pos_ref, aff_ref, x_ref, g_ref, *, cap):
    l = h_ref.shape[1]
    ge = pos_ref.shape[1]
    slot = lax.broadcasted_iota(jnp.int32, (cap, l), 0).astype(F32)
    hits = [pos_ref[0, el] == slot for el in range(ge)]
    onehot = jnp.concatenate([jnp.where(hit, 1.0, 0.0).astype(BF16) for hit in hits], axis=0)
    x = jnp.dot(onehot, h_ref[0], preferred_element_type=F32).astype(BF16)
    for el, hit in enumerate(hits):
        x_ref[el] = x[el * cap:(el + 1) * cap]
        g_ref[el] = jnp.sum(jnp.where(hit, aff_ref[0, el], 0.0), axis=1, keepdims=True)


GATHER_ROWS = 512


def _gather(hf, pos, aff_t, cap):
    b, l, d = hf.shape
    e = pos.shape[1]
    ge = max(1, min(e, GATHER_ROWS // cap))
    row = pl.BlockSpec((1, ge, 1, l), lambda i, j: (i, j, 0, 0))
    return pl.pallas_call(
        functools.partial(_gather_kernel, cap=cap),
        out_shape=(jax.ShapeDtypeStruct((e, b * cap, d), BF16), jax.ShapeDtypeStruct((e, b * cap, 1), F32)),
        grid=(b, e // ge),
        in_specs=[pl.BlockSpec((1, l, d), lambda i, j: (i, 0, 0)), row, row],
        out_specs=(pl.BlockSpec((ge, cap, d), lambda i, j: (j, i, 0)), pl.BlockSpec((ge, cap, 1), lambda i, j: (j, i, 0))),
        compiler_params=_cparams("arbitrary", "arbitrary"), name="expert_gather",
    )(hf, pos.reshape(b, e, 1, l), aff_t.reshape(b, e, 1, l))


FFN_ROWS = 512
FFN_M_TILES = 2
FFN_F_TILE = 256


def _ffn_kernel(*refs, n_sets):
    wg_ref, wu_ref, wd_ref = refs[:3]
    x_refs = refs[3:3 + n_sets]
    g_refs = refs[3 + n_sets:3 + 2 * n_sets]
    y_refs = refs[3 + 2 * n_sets:3 + 3 * n_sets]
    acc_refs = refs[3 + 3 * n_sets:]
    f = pl.program_id(2)
    wg = wg_ref[0].astype(BF16)
    wu = wu_ref[0].astype(BF16)
    wd = wd_ref[0].astype(BF16)

    @pl.when(f == 0)
    def _():
        for acc_s in acc_refs:
            acc_s[...] = jnp.zeros_like(acc_s)

    for x_ref, acc_s in zip(x_refs, acc_refs):
        tm = x_ref.shape[1]
        rc = min(FFN_ROWS, tm)
        for i in range(tm // rc):
            rows = slice(i * rc, (i + 1) * rc)
            x = x_ref[0, rows, :]
            a = jnp.dot(x, wg, preferred_element_type=F32)
            u = jnp.dot(x, wu, preferred_element_type=F32)
            hm = (_silu(a) * u).astype(BF16)
            acc_s[rows, :] += jnp.dot(hm, wd, preferred_element_type=F32)

    @pl.when(f == pl.num_programs(2) - 1)
    def _():
        for acc_s, g_ref, y_ref in zip(acc_refs, g_refs, y_refs):
            y_ref[0] = (acc_s[...] * g_ref[0]).astype(BF16)


def _expert_ffn(xes, gates, w_gate, w_up, w_down, layer):
    e, _, d = xes[0].shape
    ff = w_gate.shape[-1]
    tf = FFN_F_TILE
    tms = [xe.shape[1] // FFN_M_TILES for xe in xes]
    row = lambda i, j, f: (i, j, 0)
    w_in_blk = pl.BlockSpec((None, 1, d, tf), lambda i, j, f: (layer, i, 0, f))
    return pl.pallas_call(
        functools.partial(_ffn_kernel, n_sets=len(xes)),
        out_shape=[jax.ShapeDtypeStruct(xe.shape, BF16) for xe in xes],
        grid=(e, FFN_M_TILES, ff // tf),
        in_specs=([w_in_blk, w_in_blk, pl.BlockSpec((None, 1, tf, d), lambda i, j, f: (layer, i, f, 0))]
                  + [pl.BlockSpec((1, tm, d), row) for tm in tms]
                  + [pl.BlockSpec((1, tm, 1), row) for tm in tms]),
        out_specs=[pl.BlockSpec((1, tm, d), row) for tm in tms],
        scratch_shapes=[pltpu.VMEM((tm, d), F32) for tm in tms],
        compiler_params=_cparams("arbitrary", "arbitrary", "arbitrary"), name="expert_ffn",
    )(w_gate, w_up, w_down, *xes, *gates)


def _combine_kernel(y_ref, post_ref, x_ref, gt_ref, o_ref, *, cap):
    tt = x_ref.shape[1]
    slot = lax.broadcasted_iota(jnp.int32, (tt, cap), 1).astype(F32)
    acc = jnp.zeros((tt, x_ref.shape[2]), F32)
    for ex in range(N_EXPERTS):
        onehot = jnp.where(post_ref[0, :, ex:ex + 1] == slot, 1.0, 0.0).astype(BF16)
        acc = acc + jnp.dot(onehot, y_ref[ex], preferred_element_type=F32)
    o_ref[0] = x_ref[0] + gt_ref[0] * acc


def _combine(y, pos_t, x_mid, gt, cap):
    b, l, d = x_mid.shape
    e = y.shape[0]
    tt = min(512, l)
    mod_map = (lambda i, j: (i, 0, 0)) if gt.shape[0] == b else (lambda i, j: (0, 0, 0))
    return pl.pallas_call(
        functools.partial(_combine_kernel, cap=cap),
        out_shape=jax.ShapeDtypeStruct((b, l, d), F32),
        grid=(b, l // tt),
        in_specs=[pl.BlockSpec((e, cap, d), lambda i, j: (0, i, 0)),
                  pl.BlockSpec((1, tt, LANES), lambda i, j: (i, j, 0)),
                  pl.BlockSpec((1, tt, d), lambda i, j: (i, j, 0)),
                  pl.BlockSpec((1, 1, d), mod_map)],
        out_specs=pl.BlockSpec((1, tt, d), lambda i, j: (i, j, 0)),
        compiler_params=_cparams("arbitrary", "arbitrary"), name="expert_combine",
    )(y, pos_t, x_mid, gt)


def _rotary_tables(pos, half):
    inv = ROPE_BASE ** (-np.arange(half, dtype=np.float64) / half)
    ang = pos.astype(np.float64)[:, None] * inv[None, :]
    return np.cos(ang), np.sin(ang)


def _position_tables(seq, ctx_len):
    f = lambda a: jnp.asarray(a, F32)
    t = np.arange(seq)
    cr, sr = _rotary_tables(t // GRID_W, HEAD_DIM // 4)
    cc, sc = _rotary_tables(t % GRID_W, HEAD_DIM // 4)
    cos_a = np.tile(np.concatenate([cr, cr, cc, cc], axis=1), (1, LANES // HEAD_DIM))
    sin_a = np.tile(np.concatenate([-sr, sr, -sc, sc], axis=1), (1, LANES // HEAD_DIM))

    def ret(pos):
        c, s = _rotary_tables(pos, HEAD_DIM // 2)
        return (f(np.tile(np.concatenate([c, c], axis=1), (1, LANES // HEAD_DIM))),
                f(np.tile(np.concatenate([-s, s], axis=1), (1, LANES // HEAD_DIM))),
                f(c.T), f(s.T))

    lat = (f(cos_a), f(sin_a)) + ret(ctx_len + np.arange(seq))
    ctx = (jnp.ones((ctx_len, LANES), F32), jnp.zeros((ctx_len, LANES), F32)) + ret(np.arange(ctx_len))
    return lat, ctx


def _route(hf, aff_t):
    cap = CAPACITY_FACTOR * hf.shape[1] // N_EXPERTS
    pos, pos_t = _select(aff_t, cap)
    xe, gate = _gather(hf, pos, aff_t, cap)
    return cap, pos_t, xe, gate


def kernel(x, c, ctx, c_ctx, w_mod, b_mod, g_mix, g_ffn, w_in, q_norm_g, k_norm_g, att_sink, ret_decay_logit,
           ret_norm_g, conv_w, conv_b, conv_norm_g, conv_norm_b, w_out, w_router, w_gate, w_up, w_down):
    b, s, d = x.shape
    cl = ctx.shape[1]
    depth = w_mod.shape[0]
    tabs_l, tabs_c = _position_tables(s, cl)

    cvec = jnp.concatenate([c, c_ctx[None, :], jnp.zeros((MOD_ROWS - b - 1, d), F32)], axis=0)
    mod = _modulation(cvec, w_mod, b_mod)

    w_in_bf = w_in.astype(BF16)
    w_rk_t = jnp.swapaxes(w_in[:, :, COL_RK:COL_RK + RET_WIDTH], 1, 2).astype(BF16)
    w_out_bf = w_out.astype(BF16)
    w_router_t = jnp.swapaxes(w_router, 1, 2)
    log_g = jax.nn.log_sigmoid(ret_decay_logit.astype(F32))
    tile2 = lambda g: jnp.tile(g, LANES // HEAD_DIM)[None, :]
    zero_state = jnp.zeros((b, 2, RET_HEADS, HEAD_DIM, HEAD_DIM), F32)

    xc = ctx
    for l in range(depth):
        last = l == depth - 1
        m_l = [mod[l, :b, i * d:(i + 1) * d][:, None, :] for i in range(6)]
        m_c = [mod[l, b:b + 1, i * d:(i + 1) * d][:, None, :] for i in range(6)]
        vec = lambda a: a[l][None, :]
        qg, kg = tile2(q_norm_g[l]), tile2(k_norm_g[l])

        aq_c, ak_c, av_c, rq_c, rkt_c, rv_c, rg_c, hc_c = _project(
            xc, vec(g_mix), m_c[0], m_c[1], w_in_bf[l], w_rk_t[l], tabs_c, qg, kg)
        aq_l, ak_l, av_l, rq_l, rkt_l, rv_l, rg_l, hc_l = _project(
            x, vec(g_mix), m_l[0], m_l[1], w_in_bf[l], w_rk_t[l], tabs_l, qg, kg)

        att_l = _attention_latent(aq_l, ak_l, av_l, ak_c, av_c, att_sink[l])
        of_c, ob_c, s_ctx = _retention_scan(rq_c, rkt_c, rv_c, zero_state, log_g[l])
        of_l, ob_l, _ = _retention_scan(rq_l, rkt_l, rv_l, s_ctx, log_g[l])
        ret_l = _retention_out(of_l, ob_l, rg_l, vec(ret_norm_g))
        conv_l = _conformer_conv(hc_l, conv_w[l], vec(conv_b), vec(conv_norm_g), vec(conv_norm_b))
        x_mid, hf_l, aff_l = _out_project(att_l, ret_l, conv_l, w_out_bf[l], x, m_l[2], vec(g_ffn),
                                          m_l[3], m_l[4], w_router_t[l])
        routed = [_route(hf_l, aff_l)]
        if not last:
            att_c = _attention_context(aq_c, ak_c, av_c, att_sink[l])
            ret_c = _retention_out(of_c, ob_c, rg_c, vec(ret_norm_g))
            conv_c = _conformer_conv(hc_c, conv_w[l], vec(conv_b), vec(conv_norm_g), vec(conv_norm_b))
            xc_mid, hf_c, aff_c = _out_project(att_c, ret_c, conv_c, w_out_bf[l], xc, m_c[2], vec(g_ffn),
                                               m_c[3], m_c[4], w_router_t[l])
            routed.append(_route(hf_c, aff_c))
        ys = _expert_ffn([r[2] for r in routed], [r[3] for r in routed], w_gate, w_up, w_down, l)
        x = _combine(ys[0], routed[0][1], x_mid, m_l[5], routed[0][0])
        if not last:
            xc = _combine(ys[1], routed[1][1], xc_mid, m_c[5], routed[1][0])
    return x
```

```python
import functools

import numpy as np
import jax
import jax.numpy as jnp
from jax import lax
from jax.experimental import pallas as pl
from jax.experimental.pallas import tpu as pltpu

F32 = jnp.float32
BF16 = jnp.bfloat16
HIGHEST = lax.Precision.HIGHEST

D_MODEL = 1024
HEAD_DIM = 64
GRID_W = 64
ROPE_BASE = 10000.0
EPS = 1e-6
NEG_INF = -1e30
ATT_HEADS = 8
ATT_KV_HEADS = 2
ATT_GROUP = ATT_HEADS // ATT_KV_HEADS
ATT_WIDTH = ATT_HEADS * HEAD_DIM
KV_WIDTH = ATT_KV_HEADS * HEAD_DIM
BLOCK = 128
RET_HEADS = 4
RET_WIDTH = RET_HEADS * HEAD_DIM
RET_CHUNK = 256
CONV_WIDTH = D_MODEL - ATT_WIDTH - RET_WIDTH
CONV_KERNEL = 31
CONV_PAD = CONV_KERNEL // 2
N_EXPERTS = 16
CAPACITY_FACTOR = 2
COL_AQ = 0
COL_AK = COL_AQ + ATT_WIDTH
COL_AV = COL_AK + KV_WIDTH
COL_RQ = COL_AV + KV_WIDTH
COL_RK = COL_RQ + RET_WIDTH
COL_RV = COL_RK + RET_WIDTH
COL_RG = COL_RV + RET_WIDTH
COL_CV = COL_RG + RET_WIDTH
IN_COLS = COL_CV + 2 * CONV_WIDTH

LANES = 128
SUBLANES = 8
VMEM_LIMIT = 56 * 1024 * 1024
MOD_ROWS = 24


def _cparams(*sem):
    return pltpu.CompilerParams(dimension_semantics=sem, vmem_limit_bytes=VMEM_LIMIT)


def _sigmoid(x):
    return 1.0 / (1.0 + jnp.exp(-x))


def _silu(x):
    return x * _sigmoid(x)


def _mod_kernel(c_ref, w_ref, b_ref, o_ref):
    s = _silu(c_ref[...])
    o_ref[0] = jnp.dot(s, w_ref[0], precision=HIGHEST, preferred_element_type=F32) + b_ref[0]


def _modulation(cvec, w_mod, b_mod):
    depth, d, n = w_mod.shape
    tn = 1536
    return pl.pallas_call(
        _mod_kernel,
        out_shape=jax.ShapeDtypeStruct((depth, MOD_ROWS, n), F32),
        grid=(depth, n // tn),
        in_specs=[pl.BlockSpec((MOD_ROWS, d), lambda l, j: (0, 0)),
                  pl.BlockSpec((1, d, tn), lambda l, j: (l, 0, j)),
                  pl.BlockSpec((1, 1, tn), lambda l, j: (l, 0, j))],
        out_specs=pl.BlockSpec((1, MOD_ROWS, tn), lambda l, j: (l, 0, j)),
        compiler_params=_cparams("arbitrary", "arbitrary"),
        name="modulation",
    )(cvec, w_mod, b_mod.reshape(depth, 1, n))


def _pair_rms(u, gain):
    lane = lax.broadcasted_iota(jnp.int32, u.shape, 1)
    lo = lane < HEAD_DIM
    sq = u * u
    s_all = jnp.sum(sq, axis=1, keepdims=True)
    s_lo = jnp.sum(jnp.where(lo, sq, 0.0), axis=1, keepdims=True)
    ms = jnp.where(lo, s_lo, s_all - s_lo) * (1.0 / HEAD_DIM)
    return u * lax.rsqrt(ms + EPS) * gain


def _lane_rotary(u, cos, sins, half):
    lane = lax.broadcasted_iota(jnp.int32, u.shape, 1)
    ahead = pltpu.roll(u, LANES - half, 1)
    behind = pltpu.roll(u, half, 1)
    partner = jnp.where((lane & (2 * half - 1)) < half, ahead, behind)
    return u * cos + partner * sins


def _proj_kernel(x_ref, g_ref, sh_ref, sc_ref, w_ref, wkt_ref, cosa_ref, sina_ref, cosr_ref, sinr_ref,
                 cosk_ref, sink_ref, qg_ref, kg_ref,
                 q_ref, k_ref, v_ref, rq_ref, rkt_ref, rv_ref, rg_ref, hc_ref):
    x = x_ref[0]
    ms = jnp.mean(x * x, axis=-1, keepdims=True)
    h = x * lax.rsqrt(ms + EPS) * g_ref[...]
    h = h * (1.0 + sc_ref[0]) + sh_ref[0]
    hb = h.astype(BF16)

    def proj(c0, width):
        return jnp.dot(hb, w_ref[:, c0:c0 + width], preferred_element_type=F32)

    cosa, sina = cosa_ref[...], sina_ref[...]
    u = proj(COL_AQ, ATT_WIDTH)
    for j in range(ATT_WIDTH // LANES):
        t = _pair_rms(u[:, j * LANES:(j + 1) * LANES], qg_ref[...])
        t = _lane_rotary(t, cosa, sina, HEAD_DIM // 4)
        q_ref[0, :, j * LANES:(j + 1) * LANES] = (t * HEAD_DIM ** -0.5).astype(BF16)
    u = proj(COL_AK, 2 * KV_WIDTH)
    t = _pair_rms(u[:, :KV_WIDTH], kg_ref[...])
    k_ref[0] = _lane_rotary(t, cosa, sina, HEAD_DIM // 4).astype(BF16)
    v_ref[0] = u[:, KV_WIDTH:].astype(BF16)
    cosr, sinr = cosr_ref[...], sinr_ref[...]
    u = proj(COL_RQ, RET_WIDTH)
    for j in range(RET_WIDTH // LANES):
        t = _lane_rotary(u[:, j * LANES:(j + 1) * LANES], cosr, sinr, HEAD_DIM // 2)
        rq_ref[0, :, j * LANES:(j + 1) * LANES] = t.astype(BF16)
    rv_ref[0] = proj(COL_RV, RET_WIDTH).astype(BF16)
    rg_ref[0] = proj(COL_RG, RET_WIDTH)
    ut = lax.dot_general(wkt_ref[...], hb, (((1,), (1,)), ((), ())), preferred_element_type=F32)
    cosk, sink = cosk_ref[...], sink_ref[...]
    half = HEAD_DIM // 2
    for hd in range(RET_HEADS):
        x1 = ut[hd * HEAD_DIM:hd * HEAD_DIM + half]
        x2 = ut[hd * HEAD_DIM + half:(hd + 1) * HEAD_DIM]
        rkt_ref[0, hd * HEAD_DIM:hd * HEAD_DIM + half, :] = ((x1 * cosk - x2 * sink) * HEAD_DIM ** -0.5).astype(BF16)
        rkt_ref[0, hd * HEAD_DIM + half:(hd + 1) * HEAD_DIM, :] = ((x1 * sink + x2 * cosk) * HEAD_DIM ** -0.5).astype(BF16)
    u = proj(COL_CV, 2 * CONV_WIDTH)
    hc_ref[0] = u[:, :CONV_WIDTH] * _sigmoid(u[:, CONV_WIDTH:])


def _project(x, g, sh, sc, w_bf, wkt_bf, tabs, qg, kg):
    b, l, d = x.shape
    tm = min(512, l)
    cosa, sina, cosr, sinr, cosk, sink = tabs
    mod_map = (lambda i, j: (i, 0, 0)) if sh.shape[0] == b else (lambda i, j: (0, 0, 0))
    row = lambda i, j: (i, j, 0)
    tab = lambda i, j: (j, 0)
    const = lambda i, j: (0, 0)
    out_shape = (
        jax.ShapeDtypeStruct((b, l, ATT_WIDTH), BF16),
        jax.ShapeDtypeStruct((b, l, KV_WIDTH), BF16),
        jax.ShapeDtypeStruct((b, l, KV_WIDTH), BF16),
        jax.ShapeDtypeStruct((b, l, RET_WIDTH), BF16),
        jax.ShapeDtypeStruct((b, RET_WIDTH, l), BF16),
        jax.ShapeDtypeStruct((b, l, RET_WIDTH), BF16),
        jax.ShapeDtypeStruct((b, l, RET_WIDTH), F32),
        jax.ShapeDtypeStruct((b, l, CONV_WIDTH), F32),
    )
    out_specs = (
        pl.BlockSpec((1, tm, ATT_WIDTH), row),
        pl.BlockSpec((1, tm, KV_WIDTH), row),
        pl.BlockSpec((1, tm, KV_WIDTH), row),
        pl.BlockSpec((1, tm, RET_WIDTH), row),
        pl.BlockSpec((1, RET_WIDTH, tm), lambda i, j: (i, 0, j)),
        pl.BlockSpec((1, tm, RET_WIDTH), row),
        pl.BlockSpec((1, tm, RET_WIDTH), row),
        pl.BlockSpec((1, tm, CONV_WIDTH), row),
    )
    in_specs = [
        pl.BlockSpec((1, tm, d), row),
        pl.BlockSpec((1, d), const),
        pl.BlockSpec((1, 1, d), mod_map),
        pl.BlockSpec((1, 1, d), mod_map),
        pl.BlockSpec((d, IN_COLS), const),
        pl.BlockSpec((RET_WIDTH, d), const),
        pl.BlockSpec((tm, LANES), tab),
        pl.BlockSpec((tm, LANES), tab),
        pl.BlockSpec((tm, LANES), tab),
        pl.BlockSpec((tm, LANES), tab),
        pl.BlockSpec((HEAD_DIM // 2, tm), lambda i, j: (0, j)),
        pl.BlockSpec((HEAD_DIM // 2, tm), lambda i, j: (0, j)),
        pl.BlockSpec((1, LANES), const),
        pl.BlockSpec((1, LANES), const),
    ]
    return pl.pallas_call(
        _proj_kernel, out_shape=out_shape, grid=(b, l // tm), in_specs=in_specs, out_specs=out_specs,
        compiler_params=_cparams("arbitrary", "arbitrary"), name="in_proj",
    )(x, g, sh, sc, w_bf, wkt_bf, cosa, sina, cosr, sinr, cosk, sink, qg, kg)


def _softmax_pv(qs, pieces, sink_col):
    scores = []
    for k, _, mask in pieces:
        s = lax.dot_general(qs, k, (((1,), (1,)), ((), ())), preferred_element_type=F32)
        if mask is not None:
            s = jnp.where(mask, s, NEG_INF)
        scores.append(s)
    m_vec = None
    for s in scores:
        for j in range(s.shape[1] // LANES):
            chunk = s[:, j * LANES:(j + 1) * LANES]
            m_vec = chunk if m_vec is None else jnp.maximum(m_vec, chunk)
    m = jnp.maximum(sink_col, jnp.max(m_vec, axis=1, keepdims=True))
    p_vec = None
    acc = None
    for s, (_, v, _) in zip(scores, pieces):
        p = jnp.exp(s - m)
        for j in range(p.shape[1] // LANES):
            chunk = p[:, j * LANES:(j + 1) * LANES]
            p_vec = chunk if p_vec is None else p_vec + chunk
        o = jnp.dot(p.astype(BF16), v, preferred_element_type=F32)
        acc = o if acc is None else acc + o
    den = jnp.exp(sink_col - m) + jnp.sum(p_vec, axis=1, keepdims=True)
    return acc / den


def _sink_column(sink_ref, kh, rows_per_head):
    row = lax.broadcasted_iota(jnp.int32, (ATT_GROUP * rows_per_head, 1), 0)
    col = jnp.full(row.shape, sink_ref[kh * ATT_GROUP + ATT_GROUP - 1], F32)
    for g in range(ATT_GROUP - 2, -1, -1):
        col = jnp.where(row < (g + 1) * rows_per_head, sink_ref[kh * ATT_GROUP + g], col)
    return col


def _stack_heads(q_ref, kh):
    return jnp.concatenate(
        [q_ref[0, :, (kh * ATT_GROUP + g) * HEAD_DIM:(kh * ATT_GROUP + g + 1) * HEAD_DIM] for g in range(ATT_GROUP)],
        axis=0)


def _unstack_heads(o_ref, kh, out, rows_per_head):
    for g in range(ATT_GROUP):
        hd = kh * ATT_GROUP + g
        o_ref[0, :, hd * HEAD_DIM:(hd + 1) * HEAD_DIM] = out[g * rows_per_head:(g + 1) * rows_per_head].astype(BF16)


def _att_lat_kernel(sink_ref, q_ref, kp_ref, kc_ref, kn_ref, vp_ref, vc_ref, vn_ref, kx_ref, vx_ref, o_ref):
    n = pl.program_id(1)
    nb = pl.num_programs(1)
    shape = (ATT_GROUP * BLOCK, BLOCK)
    a = lax.broadcasted_iota(jnp.int32, shape, 0) & (BLOCK - 1)
    w = lax.broadcasted_iota(jnp.int32, shape, 1)
    far = 4 * BLOCK
    mask_p = w >= a + jnp.where(n > 0, 0, far)
    mask_n = w <= a - jnp.where(n < nb - 1, 0, far)
    for kh in range(ATT_KV_HEADS):
        sl = slice(kh * HEAD_DIM, (kh + 1) * HEAD_DIM)
        pieces = [(kp_ref[0, :, sl], vp_ref[0, :, sl], mask_p),
                  (kc_ref[0, :, sl], vc_ref[0, :, sl], None),
                  (kn_ref[0, :, sl], vn_ref[0, :, sl], mask_n),
                  (kx_ref[0, :, sl], vx_ref[0, :, sl], None)]
        out = _softmax_pv(_stack_heads(q_ref, kh), pieces, _sink_column(sink_ref, kh, BLOCK))
        _unstack_heads(o_ref, kh, out, BLOCK)


def _attention_latent(q, k, v, kx, vx, sink):
    b, s, _ = q.shape
    c = kx.shape[1]
    nb = s // BLOCK
    cur = lambda i, n: (i, n, 0)
    prv = lambda i, n: (i, jnp.maximum(n - 1, 0), 0)
    nxt = lambda i, n: (i, jnp.minimum(n + 1, nb - 1), 0)
    ctx = lambda i, n: (i, 0, 0)
    kv = lambda m: pl.BlockSpec((1, BLOCK, KV_WIDTH), m)
    return pl.pallas_call(
        _att_lat_kernel,
        out_shape=jax.ShapeDtypeStruct((b, s, ATT_WIDTH), BF16),
        grid=(b, nb),
        in_specs=[pl.BlockSpec(memory_space=pltpu.SMEM),
                  pl.BlockSpec((1, BLOCK, ATT_WIDTH), cur),
                  kv(prv), kv(cur), kv(nxt), kv(prv), kv(cur), kv(nxt),
                  pl.BlockSpec((1, c, KV_WIDTH), ctx), pl.BlockSpec((1, c, KV_WIDTH), ctx)],
        out_specs=pl.BlockSpec((1, BLOCK, ATT_WIDTH), cur),
        compiler_params=_cparams("arbitrary", "arbitrary"), name="att_latent",
    )(sink, q, k, k, k, v, v, v, kx, vx)


def _att_ctx_kernel(sink_ref, q_ref, k_ref, v_ref, o_ref):
    c = q_ref.shape[1]
    for kh in range(ATT_KV_HEADS):
        sl = slice(kh * HEAD_DIM, (kh + 1) * HEAD_DIM)
        out = _softmax_pv(_stack_heads(q_ref, kh), [(k_ref[0, :, sl], v_ref[0, :, sl], None)],
                          _sink_column(sink_ref, kh, c))
        _unstack_heads(o_ref, kh, out, c)


def _attention_context(q, k, v, sink):
    b, c, _ = q.shape
    blk = lambda w: pl.BlockSpec((1, c, w), lambda i: (i, 0, 0))
    return pl.pallas_call(
        _att_ctx_kernel,
        out_shape=jax.ShapeDtypeStruct((b, c, ATT_WIDTH), BF16),
        grid=(b,),
        in_specs=[pl.BlockSpec(memory_space=pltpu.SMEM), blk(ATT_WIDTH), blk(KV_WIDTH), blk(KV_WIDTH)],
        out_specs=blk(ATT_WIDTH),
        compiler_params=_cparams("arbitrary"), name="att_context",
    )(sink, q, k, v)


def _ret_scan_kernel(lg_ref, s0_ref, qf_ref, ktf_ref, vf_ref, qb_ref, ktb_ref, vb_ref,
                     of_ref, ob_ref, sfin_ref, st_s):
    c = pl.program_id(1)
    t = RET_CHUNK

    @pl.when(c == 0)
    def _():
        st_s[...] = s0_ref[0]

    i_col = lax.broadcasted_iota(jnp.int32, (t, HEAD_DIM), 0).astype(F32)
    j_row = lax.broadcasted_iota(jnp.int32, (HEAD_DIM, t), 1).astype(F32)
    ii = lax.broadcasted_iota(jnp.int32, (t, t), 0)
    jj = lax.broadcasted_iota(jnp.int32, (t, t), 1)
    dist = (ii - jj).astype(F32)
    for hd in range(RET_HEADS):
        lgf = lg_ref[0, hd]
        lgb = lg_ref[1, hd]
        sl = slice(hd * HEAD_DIM, (hd + 1) * HEAD_DIM)
        q = qf_ref[0, :, sl]
        kt = ktf_ref[0, sl, :]
        v = vf_ref[0, :, sl]
        dec = jnp.where(ii >= jj, jnp.exp(lgf * jnp.maximum(dist, 0.0)), jnp.exp(lgb * jnp.maximum(-dist, 0.0)))
        s = jnp.dot(q, kt, preferred_element_type=F32) * dec
        o = jnp.dot(s.astype(BF16), v, preferred_element_type=F32)
        st = st_s[0, hd]
        qd = (q.astype(F32) * jnp.exp(lgf * (i_col + 1.0))).astype(BF16)
        of_ref[0, :, sl] = o + jnp.dot(qd, st.astype(BF16), preferred_element_type=F32)
        kd = (kt.astype(F32) * jnp.exp(lgf * (t - 1.0 - j_row))).astype(BF16)
        st_s[0, hd] = jnp.exp(lgf * t) * st + jnp.dot(kd, v, preferred_element_type=F32)
        q = qb_ref[0, :, sl]
        kt = ktb_ref[0, sl, :]
        v = vb_ref[0, :, sl]
        st = st_s[1, hd]
        qd = (q.astype(F32) * jnp.exp(lgb * (t - i_col))).astype(BF16)
        ob_ref[0, :, sl] = jnp.dot(qd, st.astype(BF16), preferred_element_type=F32)
        kd = (kt.astype(F32) * jnp.exp(lgb * j_row)).astype(BF16)
        st_s[1, hd] = jnp.exp(lgb * t) * st + jnp.dot(kd, v, preferred_element_type=F32)

    @pl.when(c == pl.num_programs(1) - 1)
    def _():
        sfin_ref[0] = st_s[...]


def _retention_scan(q, kt, v, s0, log_g):
    b, l, _ = q.shape
    n = l // RET_CHUNK
    fwd = lambda i, c: (i, c, 0)
    bwd = lambda i, c: (i, n - 1 - c, 0)
    fwd_t = lambda i, c: (i, 0, c)
    bwd_t = lambda i, c: (i, 0, n - 1 - c)
    st_map = lambda i, c: (i, 0, 0, 0, 0)
    tok = lambda m: pl.BlockSpec((1, RET_CHUNK, RET_WIDTH), m)
    feat = lambda m: pl.BlockSpec((1, RET_WIDTH, RET_CHUNK), m)
    st_blk = pl.BlockSpec((1, 2, RET_HEADS, HEAD_DIM, HEAD_DIM), st_map)
    return pl.pallas_call(
        _ret_scan_kernel,
        out_shape=(jax.ShapeDtypeStruct((b, l, RET_WIDTH), F32),
                   jax.ShapeDtypeStruct((b, l, RET_WIDTH), F32),
                   jax.ShapeDtypeStruct((b, 2, RET_HEADS, HEAD_DIM, HEAD_DIM), F32)),
        grid=(b, n),
        in_specs=[pl.BlockSpec(memory_space=pltpu.SMEM), st_blk,
                  tok(fwd), feat(fwd_t), tok(fwd), tok(bwd), feat(bwd_t), tok(bwd)],
        out_specs=(tok(fwd), tok(bwd), st_blk),
        scratch_shapes=[pltpu.VMEM((2, RET_HEADS, HEAD_DIM, HEAD_DIM), F32)],
        compiler_params=_cparams("arbitrary", "arbitrary"), name="retention_scan",
    )(log_g, s0, q, kt, v, q, kt, v)


def _ret_out_kernel(of_ref, ob_ref, gate_ref, g_ref, o_ref):
    for j in range(RET_WIDTH // LANES):
        sl = slice(j * LANES, (j + 1) * LANES)
        o = of_ref[0, :, sl] + ob_ref[0, :, sl]
        lane = lax.broadcasted_iota(jnp.int32, o.shape, 1)
        lo = lane < HEAD_DIM

        def pair_mean(z):
            s_all = jnp.sum(z, axis=1, keepdims=True)
            s_lo = jnp.sum(jnp.where(lo, z, 0.0), axis=1, keepdims=True)
            return jnp.where(lo, s_lo, s_all - s_lo) * (1.0 / HEAD_DIM)

        cen = o - pair_mean(o)
        var = pair_mean(cen * cen)
        y = cen * lax.rsqrt(var + EPS) * g_ref[:, sl]
        o_ref[0, :, sl] = (y * _silu(gate_ref[0, :, sl])).astype(BF16)


def _retention_out(o_f, o_b, gate, g):
    b, l, w = o_f.shape
    tm = min(512, l)
    blk = pl.BlockSpec((1, tm, w), lambda i, j: (i, j, 0))
    return pl.pallas_call(
        _ret_out_kernel,
        out_shape=jax.ShapeDtypeStruct((b, l, w), BF16),
        grid=(b, l // tm),
        in_specs=[blk, blk, blk, pl.BlockSpec((1, w), lambda i, j: (0, 0))],
        out_specs=blk,
        compiler_params=_cparams("arbitrary", "arbitrary"), name="retention_out",
    )(o_f, o_b, gate, g)


CONV_HALO = 16
CONV_ROWS = 128
CONV_COPY_ROWS = 256


def _conv_kernel(h_ref, w_ref, b_ref, g_ref, beta_ref, o_ref, pad_s, shift_s):
    l = h_ref.shape[1]
    zeros = jnp.zeros((CONV_HALO, CONV_WIDTH), F32)
    pad_s[0:CONV_HALO, :] = zeros
    pad_s[CONV_HALO + l:2 * CONV_HALO + l, :] = zeros
    pad_s[CONV_HALO:CONV_HALO + l, :] = h_ref[0]
    ext = shift_s.shape[1]
    for r in range(1, SUBLANES):
        for c0 in range(0, ext, CONV_COPY_ROWS):
            n = min(CONV_COPY_ROWS, ext - c0)
            shift_s[r - 1, c0:c0 + n, :] = pad_s[c0 + r:c0 + r + n, :]
    for r0 in range(0, l, CONV_ROWS):
        acc = jnp.zeros((CONV_ROWS, CONV_WIDTH), F32) + b_ref[...]
        for j in range(CONV_KERNEL):
            off = j + CONV_HALO - CONV_PAD
            start = r0 + (off // SUBLANES) * SUBLANES
            if off % SUBLANES == 0:
                rows = pad_s[start:start + CONV_ROWS, :]
            else:
                rows = shift_s[off % SUBLANES - 1, start:start + CONV_ROWS, :]
            acc = acc + rows * w_ref[j:j + 1, :]
        mu = jnp.mean(acc, axis=1, keepdims=True)
        cen = acc - mu
        var = jnp.mean(cen * cen, axis=1, keepdims=True)
        y = cen * lax.rsqrt(var + EPS) * g_ref[...] + beta_ref[...]
        o_ref[0, r0:r0 + CONV_ROWS, :] = _silu(y).astype(BF16)


def _conformer_conv(hc, w, bias, g, beta):
    b, l, cw = hc.shape
    blk = pl.BlockSpec((1, l, cw), lambda i: (i, 0, 0))
    vec = pl.BlockSpec((1, cw), lambda i: (0, 0))
    return pl.pallas_call(
        _conv_kernel,
        out_shape=jax.ShapeDtypeStruct((b, l, cw), BF16),
        grid=(b,),
        in_specs=[blk, pl.BlockSpec((CONV_KERNEL, cw), lambda i: (0, 0)), vec, vec, vec],
        out_specs=blk,
        scratch_shapes=[pltpu.VMEM((l + 2 * CONV_HALO, cw), F32),
                        pltpu.VMEM((SUBLANES - 1, l + 2 * CONV_HALO - SUBLANES, cw), F32)],
        compiler_params=_cparams("arbitrary"), name="conformer_conv",
    )(hc, w, bias, g, beta)


OUT_SUB_ROWS = 256


def _out_kernel(att_ref, ret_ref, conv_ref, w_ref, x_ref, gt_ref, g_ref, sh_ref, sc_ref, wr_ref,
                xmid_ref, hf_ref, aff_ref):
    tm = x_ref.shape[1]
    sub = min(OUT_SUB_ROWS, tm)
    for i in range(tm // sub):
        rows = slice(i * sub, (i + 1) * sub)
        mix = jnp.dot(att_ref[0, rows, :], w_ref[0:ATT_WIDTH, :], preferred_element_type=F32)
        mix = mix + jnp.dot(ret_ref[0, rows, :], w_ref[ATT_WIDTH:ATT_WIDTH + RET_WIDTH, :],
                            preferred_element_type=F32)
        mix = mix + jnp.dot(conv_ref[0, rows, :], w_ref[ATT_WIDTH + RET_WIDTH:, :], preferred_element_type=F32)
        xm = x_ref[0, rows, :] + gt_ref[0] * mix
        xmid_ref[0, rows, :] = xm
        ms = jnp.mean(xm * xm, axis=-1, keepdims=True)
        h = xm * lax.rsqrt(ms + EPS) * g_ref[...]
        h = h * (1.0 + sc_ref[0]) + sh_ref[0]
        hf_ref[0, rows, :] = h.astype(BF16)
        lt = lax.dot_general(wr_ref[...], h, (((1,), (1,)), ((), ())), precision=HIGHEST,
                             preferred_element_type=F32)
        e = jnp.exp(lt - jnp.max(lt, axis=0, keepdims=True))
        aff_ref[0, :, rows] = e / jnp.sum(e, axis=0, keepdims=True)


def _out_project(att, ret, conv, w_bf, x, gt, g, sh, sc, wr_t):
    b, l, d = x.shape
    tm = min(512, l)
    mod_map = (lambda i, j: (i, 0, 0)) if gt.shape[0] == b else (lambda i, j: (0, 0, 0))
    row = lambda i, j: (i, j, 0)
    const = lambda i, j: (0, 0)
    mod = pl.BlockSpec((1, 1, d), mod_map)
    return pl.pallas_call(
        _out_kernel,
        out_shape=(jax.ShapeDtypeStruct((b, l, d), F32),
                   jax.ShapeDtypeStruct((b, l, d), BF16),
                   jax.ShapeDtypeStruct((b, N_EXPERTS, l), F32)),
        grid=(b, l // tm),
        in_specs=[pl.BlockSpec((1, tm, ATT_WIDTH), row), pl.BlockSpec((1, tm, RET_WIDTH), row),
                  pl.BlockSpec((1, tm, CONV_WIDTH), row), pl.BlockSpec((d, d), const),
                  pl.BlockSpec((1, tm, d), row), mod, pl.BlockSpec((1, d), const), mod, mod,
                  pl.BlockSpec((N_EXPERTS, d), const)],
        out_specs=(pl.BlockSpec((1, tm, d), row), pl.BlockSpec((1, tm, d), row),
                   pl.BlockSpec((1, N_EXPERTS, tm), lambda i, j: (i, 0, j))),
        compiler_params=_cparams("arbitrary", "arbitrary"), name="out_proj",
    )(att, ret, conv, w_bf, x, gt, g, sh, sc, wr_t)


def _prefix_sum_lanes(x):
    l = x.shape[1]
    lane = lax.broadcasted_iota(jnp.int32, x.shape, 1)
    k = 1
    while k < l:
        x = x + jnp.where(lane >= k, pltpu.roll(x, k, 1), 0.0)
        k *= 2
    return x


def _select_kernel(aff_ref, pos_ref, post_ref, *, cap):
    for s in range(aff_ref.shape[0]):
        _select_one(aff_ref.at[s], pos_ref.at[s], post_ref.at[s], cap)


def _select_one(aff_ref, pos_ref, post_ref, cap):
    aff = aff_ref[...]
    thr = jnp.zeros((aff.shape[0], 1), jnp.int32)
    for bit in range(30, -1, -1):
        cand = thr | (1 << bit)
        cnt = jnp.sum(jnp.where(aff >= pltpu.bitcast(cand, F32), 1.0, 0.0), axis=1, keepdims=True)
        thr = jnp.where(cnt >= cap, cand, thr)
    above = aff >= pltpu.bitcast(thr + 1, F32)
    tie = (aff >= pltpu.bitcast(thr, F32)) & jnp.logical_not(above)
    n_above = jnp.sum(jnp.where(above, 1.0, 0.0), axis=1, keepdims=True)
    tie_f = jnp.where(tie, 1.0, 0.0)
    tie_rank = _prefix_sum_lanes(tie_f) - tie_f
    sel = above | (tie & (tie_rank < cap - n_above))
    sel_f = jnp.where(sel, 1.0, 0.0)
    pos = jnp.where(sel, _prefix_sum_lanes(sel_f) - sel_f, -1.0)
    pos_ref[...] = pos
    padded = jnp.concatenate([pos, jnp.full((LANES - pos.shape[0], pos.shape[1]), -1.0, F32)], axis=0)
    post_ref[...] = padded.T


SELECT_SAMPLES = 4


def _select(aff_t, cap):
    b, e, l = aff_t.shape
    sb = SELECT_SAMPLES if b % SELECT_SAMPLES == 0 else 1
    return pl.pallas_call(
        functools.partial(_select_kernel, cap=cap),
        out_shape=(jax.ShapeDtypeStruct((b, e, l), F32), jax.ShapeDtypeStruct((b, l, LANES), F32)),
        grid=(b // sb,),
        in_specs=[pl.BlockSpec((sb, e, l), lambda i: (i, 0, 0))],
        out_specs=(pl.BlockSpec((sb, e, l), lambda i: (i, 0, 0)), pl.BlockSpec((sb, l, LANES), lambda i: (i, 0, 0))),
        compiler_params=_cparams("arbitrary"), name="expert_select",
    )(aff_t)


def _gather_kernel(h_ref, pos_ref, aff_ref, x_ref, g_ref, *, cap):
    l = h_ref.shape[1]
    ge = pos_ref.shape[1]
    slot = lax.broadcasted_iota(jnp.int32, (cap, l), 0).astype(F32)
    hits = [pos_ref[0, el] == slot for el in range(ge)]
    onehot = jnp.concatenate([jnp.where(hit, 1.0, 0.0).astype(BF16) for hit in hits], axis=0)
    x = jnp.dot(onehot, h_ref[0], preferred_element_type=F32).astype(BF16)
    for el, hit in enumerate(hits):
        x_ref[el] = x[el * cap:(el + 1) * cap]
        g_ref[el] = jnp.sum(jnp.where(hit, aff_ref[0, el], 0.0), axis=1, keepdims=True)


GATHER_ROWS = 512


def _gather(hf, pos, aff_t, cap):
    b, l, d = hf.shape
    e = pos.shape[1]
    ge = max(1, min(e, GATHER_ROWS // cap))
    row = pl.BlockSpec((1, ge, 1, l), lambda i, j: (i, j, 0, 0))
    return pl.pallas_call(
        functools.partial(_gather_kernel, cap=cap),
        out_shape=(jax.ShapeDtypeStruct((e, b * cap, d), BF16), jax.ShapeDtypeStruct((e, b * cap, 1), F32)),
        grid=(b, e // ge),
        in_specs=[pl.BlockSpec((1, l, d), lambda i, j: (i, 0, 0)), row, row],
        out_specs=(pl.BlockSpec((ge, cap, d), lambda i, j: (j, i, 0)), pl.BlockSpec((ge, cap, 1), lambda i, j: (j, i, 0))),
        compiler_params=_cparams("arbitrary", "arbitrary"), name="expert_gather",
    )(hf, pos.reshape(b, e, 1, l), aff_t.reshape(b, e, 1, l))


FFN_ROWS = 512
FFN_M_TILES = 2
FFN_F_TILE = 256


def _ffn_kernel(*refs, n_sets):
    wg_ref, wu_ref, wd_ref = refs[:3]
    x_refs = refs[3:3 + n_sets]
    g_refs = refs[3 + n_sets:3 + 2 * n_sets]
    y_refs = refs[3 + 2 * n_sets:3 + 3 * n_sets]
    acc_refs = refs[3 + 3 * n_sets:]
    f = pl.program_id(2)
    wg = wg_ref[0].astype(BF16)
    wu = wu_ref[0].astype(BF16)
    wd = wd_ref[0].astype(BF16)

    @pl.when(f == 0)
    def _():
        for acc_s in acc_refs:
            acc_s[...] = jnp.zeros_like(acc_s)

    for x_ref, acc_s in zip(x_refs, acc_refs):
        tm = x_ref.shape[1]
        rc = min(FFN_ROWS, tm)
        for i in range(tm // rc):
            rows = slice(i * rc, (i + 1) * rc)
            x = x_ref[0, rows, :]
            a = jnp.dot(x, wg, preferred_element_type=F32)
            u = jnp.dot(x, wu, preferred_element_type=F32)
            hm = (_silu(a) * u).astype(BF16)
            acc_s[rows, :] += jnp.dot(hm, wd, preferred_element_type=F32)

    @pl.when(f == pl.num_programs(2) - 1)
    def _():
        for acc_s, g_ref, y_ref in zip(acc_refs, g_refs, y_refs):
            y_ref[0] = (acc_s[...] * g_ref[0]).astype(BF16)


def _expert_ffn(xes, gates, w_gate, w_up, w_down, layer):
    e, _, d = xes[0].shape
    ff = w_gate.shape[-1]
    tf = FFN_F_TILE
    tms = [xe.shape[1] // FFN_M_TILES for xe in xes]
    row = lambda i, j, f: (i, j, 0)
    w_in_blk = pl.BlockSpec((None, 1, d, tf), lambda i, j, f: (layer, i, 0, f))
    return pl.pallas_call(
        functools.partial(_ffn_kernel, n_sets=len(xes)),
        out_shape=[jax.ShapeDtypeStruct(xe.shape, BF16) for xe in xes],
        grid=(e, FFN_M_TILES, ff // tf),
        in_specs=([w_in_blk, w_in_blk, pl.BlockSpec((None, 1, tf, d), lambda i, j, f: (layer, i, f, 0))]
                  + [pl.BlockSpec((1, tm, d), row) for tm in tms]
                  + [pl.BlockSpec((1, tm, 1), row) for tm in tms]),
        out_specs=[pl.BlockSpec((1, tm, d), row) for tm in tms],
        scratch_shapes=[pltpu.VMEM((tm, d), F32) for tm in tms],
        compiler_params=_cparams("arbitrary", "arbitrary", "arbitrary"), name="expert_ffn",
    )(w_gate, w_up, w_down, *xes, *gates)


def _combine_kernel(y_ref, post_ref, x_ref, gt_ref, o_ref, *, cap):
    tt = x_ref.shape[1]
    slot = lax.broadcasted_iota(jnp.int32, (tt, cap), 1).astype(F32)
    acc = jnp.zeros((tt, x_ref.shape[2]), F32)
    for ex in range(N_EXPERTS):
        onehot = jnp.where(post_ref[0, :, ex:ex + 1] == slot, 1.0, 0.0).astype(BF16)
        acc = acc + jnp.dot(onehot, y_ref[ex], preferred_element_type=F32)
    o_ref[0] = x_ref[0] + gt_ref[0] * acc


def _combine(y, pos_t, x_mid, gt, cap):
    b, l, d = x_mid.shape
    e = y.shape[0]
    tt = min(512, l)
    mod_map = (lambda i, j: (i, 0, 0)) if gt.shape[0] == b else (lambda i, j: (0, 0, 0))
    return pl.pallas_call(
        functools.partial(_combine_kernel, cap=cap),
        out_shape=jax.ShapeDtypeStruct((b, l, d), F32),
        grid=(b, l // tt),
        in_specs=[pl.BlockSpec((e, cap, d), lambda i, j: (0, i, 0)),
                  pl.BlockSpec((1, tt, LANES), lambda i, j: (i, j, 0)),
                  pl.BlockSpec((1, tt, d), lambda i, j: (i, j, 0)),
                  pl.BlockSpec((1, 1, d), mod_map)],
        out_specs=pl.BlockSpec((1, tt, d), lambda i, j: (i, j, 0)),
        compiler_params=_cparams("arbitrary", "arbitrary"), name="expert_combine",
    )(y, pos_t, x_mid, gt)


def _rotary_tables(pos, half):
    inv = ROPE_BASE ** (-np.arange(half, dtype=np.float64) / half)
    ang = pos.astype(np.float64)[:, None] * inv[None, :]
    return np.cos(ang), np.sin(ang)


def _position_tables(seq, ctx_len):
    f = lambda a: jnp.asarray(a, F32)
    t = np.arange(seq)
    cr, sr = _rotary_tables(t // GRID_W, HEAD_DIM // 4)
    cc, sc = _rotary_tables(t % GRID_W, HEAD_DIM // 4)
    cos_a = np.tile(np.concatenate([cr, cr, cc, cc], axis=1), (1, LANES // HEAD_DIM))
    sin_a = np.tile(np.concatenate([-sr, sr, -sc, sc], axis=1), (1, LANES // HEAD_DIM))

    def ret(pos):
        c, s = _rotary_tables(pos, HEAD_DIM // 2)
        return (f(np.tile(np.concatenate([c, c], axis=1), (1, LANES // HEAD_DIM))),
                f(np.tile(np.concatenate([-s, s], axis=1), (1, LANES // HEAD_DIM))),
                f(c.T), f(s.T))

    lat = (f(cos_a), f(sin_a)) + ret(ctx_len + np.arange(seq))
    ctx = (jnp.ones((ctx_len, LANES), F32), jnp.zeros((ctx_len, LANES), F32)) + ret(np.arange(ctx_len))
    return lat, ctx


def _route(hf, aff_t):
    cap = CAPACITY_FACTOR * hf.shape[1] // N_EXPERTS
    pos, pos_t = _select(aff_t, cap)
    xe, gate = _gather(hf, pos, aff_t, cap)
    return cap, pos_t, xe, gate


def kernel(x, c, ctx, c_ctx, w_mod, b_mod, g_mix, g_ffn, w_in, q_norm_g, k_norm_g, att_sink, ret_decay_logit,
           ret_norm_g, conv_w, conv_b, conv_norm_g, conv_norm_b, w_out, w_router, w_gate, w_up, w_down):
    b, s, d = x.shape
    cl = ctx.shape[1]
    depth = w_mod.shape[0]
    tabs_l, tabs_c = _position_tables(s, cl)

    cvec = jnp.concatenate([c, c_ctx[None, :], jnp.zeros((MOD_ROWS - b - 1, d), F32)], axis=0)
    mod = _modulation(cvec, w_mod, b_mod)

    w_in_bf = w_in.astype(BF16)
    w_rk_t = jnp.swapaxes(w_in[:, :, COL_RK:COL_RK + RET_WIDTH], 1, 2).astype(BF16)
    w_out_bf = w_out.astype(BF16)
    w_router_t = jnp.swapaxes(w_router, 1, 2)
    log_g = jax.nn.log_sigmoid(ret_decay_logit.astype(F32))
    tile2 = lambda g: jnp.tile(g, LANES // HEAD_DIM)[None, :]
    zero_state = jnp.zeros((b, 2, RET_HEADS, HEAD_DIM, HEAD_DIM), F32)

    xc = ctx
    for l in range(depth):
        last = l == depth - 1
        m_l = [mod[l, :b, i * d:(i + 1) * d][:, None, :] for i in range(6)]
        m_c = [mod[l, b:b + 1, i * d:(i + 1) * d][:, None, :] for i in range(6)]
        vec = lambda a: a[l][None, :]
        qg, kg = tile2(q_norm_g[l]), tile2(k_norm_g[l])

        aq_c, ak_c, av_c, rq_c, rkt_c, rv_c, rg_c, hc_c = _project(
            xc, vec(g_mix), m_c[0], m_c[1], w_in_bf[l], w_rk_t[l], tabs_c, qg, kg)
        aq_l, ak_l, av_l, rq_l, rkt_l, rv_l, rg_l, hc_l = _project(
            x, vec(g_mix), m_l[0], m_l[1], w_in_bf[l], w_rk_t[l], tabs_l, qg, kg)

        att_l = _attention_latent(aq_l, ak_l, av_l, ak_c, av_c, att_sink[l])
        of_c, ob_c, s_ctx = _retention_scan(rq_c, rkt_c, rv_c, zero_state, log_g[l])
        of_l, ob_l, _ = _retention_scan(rq_l, rkt_l, rv_l, s_ctx, log_g[l])
        ret_l = _retention_out(of_l, ob_l, rg_l, vec(ret_norm_g))
        conv_l = _conformer_conv(hc_l, conv_w[l], vec(conv_b), vec(conv_norm_g), vec(conv_norm_b))
        x_mid, hf_l, aff_l = _out_project(att_l, ret_l, conv_l, w_out_bf[l], x, m_l[2], vec(g_ffn),
                                          m_l[3], m_l[4], w_router_t[l])
        routed = [_route(hf_l, aff_l)]
        if not last:
            att_c = _attention_context(aq_c, ak_c, av_c, att_sink[l])
            ret_c = _retention_out(of_c, ob_c, rg_c, vec(ret_norm_g))
            conv_c = _conformer_conv(hc_c, conv_w[l], vec(conv_b), vec(conv_norm_g), vec(conv_norm_b))
            xc_mid, hf_c, aff_c = _out_project(att_c, ret_c, conv_c, w_out_bf[l], xc, m_c[2], vec(g_ffn),
                                               m_c[3], m_c[4], w_router_t[l])
            routed.append(_route(hf_c, aff_c))
        ys = _expert_ffn([r[2] for r in routed], [r[3] for r in routed], w_gate, w_up, w_down, l)
        x = _combine(ys[0], routed[0][1], x_mid, m_l[5], routed[0][0])
        if not last:
            xc = _combine(ys[1], routed[1][1], xc_mid, m_c[5], routed[1][0])
    return x
```

```python
import functools

import numpy as np
import jax
import jax.numpy as jnp
from jax import lax
from jax.experimental import pallas as pl
from jax.experimental.pallas import tpu as pltpu

F32 = jnp.float32
BF16 = jnp.bfloat16
HIGHEST = lax.Precision.HIGHEST

D_MODEL = 1024
HEAD_DIM = 64
GRID_W = 64
ROPE_BASE = 10000.0
EPS = 1e-6
NEG_INF = -1e30
ATT_HEADS = 8
ATT_KV_HEADS = 2
ATT_GROUP = ATT_HEADS // ATT_KV_HEADS
ATT_WIDTH = ATT_HEADS * HEAD_DIM
KV_WIDTH = ATT_KV_HEADS * HEAD_DIM
BLOCK = 128
RET_HEADS = 4
RET_WIDTH = RET_HEADS * HEAD_DIM
RET_CHUNK = 256
CONV_WIDTH = D_MODEL - ATT_WIDTH - RET_WIDTH
CONV_KERNEL = 31
CONV_PAD = CONV_KERNEL // 2
N_EXPERTS = 16
CAPACITY_FACTOR = 2
COL_AQ = 0
COL_AK = COL_AQ + ATT_WIDTH
COL_AV = COL_AK + KV_WIDTH
COL_RQ = COL_AV + KV_WIDTH
COL_RK = COL_RQ + RET_WIDTH
COL_RV = COL_RK + RET_WIDTH
COL_RG = COL_RV + RET_WIDTH
COL_CV = COL_RG + RET_WIDTH
IN_COLS = COL_CV + 2 * CONV_WIDTH

LANES = 128
SUBLANES = 8
VMEM_LIMIT = 56 * 1024 * 1024
MOD_ROWS = 24


def _cparams(*sem):
    return pltpu.CompilerParams(dimension_semantics=sem, vmem_limit_bytes=VMEM_LIMIT)


def _sigmoid(x):
    return 1.0 / (1.0 + jnp.exp(-x))


def _silu(x):
    return x * _sigmoid(x)


def _mod_kernel(c_ref, w_ref, b_ref, o_ref):
    s = _silu(c_ref[...])
    o_ref[0] = jnp.dot(s, w_ref[0], precision=HIGHEST, preferred_element_type=F32) + b_ref[0]


def _modulation(cvec, w_mod, b_mod):
    depth, d, n = w_mod.shape
    tn = 1536
    return pl.pallas_call(
        _mod_kernel,
        out_shape=jax.ShapeDtypeStruct((depth, MOD_ROWS, n), F32),
        grid=(depth, n // tn),
        in_specs=[pl.BlockSpec((MOD_ROWS, d), lambda l, j: (0, 0)),
                  pl.BlockSpec((1, d, tn), lambda l, j: (l, 0, j)),
                  pl.BlockSpec((1, 1, tn), lambda l, j: (l, 0, j))],
        out_specs=pl.BlockSpec((1, MOD_ROWS, tn), lambda l, j: (l, 0, j)),
        compiler_params=_cparams("arbitrary", "arbitrary"),
        name="modulation",
    )(cvec, w_mod, b_mod.reshape(depth, 1, n))


def _pair_rms(u, gain):
    lane = lax.broadcasted_iota(jnp.int32, u.shape, 1)
    lo = lane < HEAD_DIM
    sq = u * u
    s_all = jnp.sum(sq, axis=1, keepdims=True)
    s_lo = jnp.sum(jnp.where(lo, sq, 0.0), axis=1, keepdims=True)
    ms = jnp.where(lo, s_lo, s_all - s_lo) * (1.0 / HEAD_DIM)
    return u * lax.rsqrt(ms + EPS) * gain


def _lane_rotary(u, cos, sins, half):
    lane = lax.broadcasted_iota(jnp.int32, u.shape, 1)
    ahead = pltpu.roll(u, LANES - half, 1)
    behind = pltpu.roll(u, half, 1)
    partner = jnp.where((lane & (2 * half - 1)) < half, ahead, behind)
    return u * cos + partner * sins


def _proj_kernel(x_ref, g_ref, sh_ref, sc_ref, w_ref, wkt_ref, cosa_ref, sina_ref, cosr_ref, sinr_ref,
                 cosk_ref, sink_ref, qg_ref, kg_ref,
                 q_ref, k_ref, v_ref, rq_ref, rkt_ref, rv_ref, rg_ref, hc_ref):
    x = x_ref[0]
    ms = jnp.mean(x * x, axis=-1, keepdims=True)
    h = x * lax.rsqrt(ms + EPS) * g_ref[...]
    h = h * (1.0 + sc_ref[0]) + sh_ref[0]
    hb = h.astype(BF16)

    def proj(c0, width):
        return jnp.dot(hb, w_ref[:, c0:c0 + width], preferred_element_type=F32)

    cosa, sina = cosa_ref[...], sina_ref[...]
    u = proj(COL_AQ, ATT_WIDTH)
    for j in range(ATT_WIDTH // LANES):
        t = _pair_rms(u[:, j * LANES:(j + 1) * LANES], qg_ref[...])
        t = _lane_rotary(t, cosa, sina, HEAD_DIM // 4)
        q_ref[0, :, j * LANES:(j + 1) * LANES] = (t * HEAD_DIM ** -0.5).astype(BF16)
    u = proj(COL_AK, 2 * KV_WIDTH)
    t = _pair_rms(u[:, :KV_WIDTH], kg_ref[...])
    k_ref[0] = _lane_rotary(t, cosa, sina, HEAD_DIM // 4).astype(BF16)
    v_ref[0] = u[:, KV_WIDTH:].astype(BF16)
    cosr, sinr = cosr_ref[...], sinr_ref[...]
    u = proj(COL_RQ, RET_WIDTH)
    for j in range(RET_WIDTH // LANES):
        t = _lane_rotary(u[:, j * LANES:(j + 1) * LANES], cosr, sinr, HEAD_DIM // 2)
        rq_ref[0, :, j * LANES:(j + 1) * LANES] = t.astype(BF16)
    rv_ref[0] = proj(COL_RV, RET_WIDTH).astype(BF16)
    rg_ref[0] = proj(COL_RG, RET_WIDTH)
    ut = lax.dot_general(wkt_ref[...], hb, (((1,), (1,)), ((), ())), preferred_element_type=F32)
    cosk, sink = cosk_ref[...], sink_ref[...]
    half = HEAD_DIM // 2
    for hd in range(RET_HEADS):
        x1 = ut[hd * HEAD_DIM:hd * HEAD_DIM + half]
        x2 = ut[hd * HEAD_DIM + half:(hd + 1) * HEAD_DIM]
        rkt_ref[0, hd * HEAD_DIM:hd * HEAD_DIM + half, :] = ((x1 * cosk - x2 * sink) * HEAD_DIM ** -0.5).astype(BF16)
        rkt_ref[0, hd * HEAD_DIM + half:(hd + 1) * HEAD_DIM, :] = ((x1 * sink + x2 * cosk) * HEAD_DIM ** -0.5).astype(BF16)
    u = proj(COL_CV, 2 * CONV_WIDTH)
    hc_ref[0] = u[:, :CONV_WIDTH] * _sigmoid(u[:, CONV_WIDTH:])


def _project(x, g, sh, sc, w_bf, wkt_bf, tabs, qg, kg):
    b, l, d = x.shape
    tm = min(512, l)
    cosa, sina, cosr, sinr, cosk, sink = tabs
    mod_map = (lambda i, j: (i, 0, 0)) if sh.shape[0] == b else (lambda i, j: (0, 0, 0))
    row = lambda i, j: (i, j, 0)
    tab = lambda i, j: (j, 0)
    const = lambda i, j: (0, 0)
    out_shape = (
        jax.ShapeDtypeStruct((b, l, ATT_WIDTH), BF16),
        jax.ShapeDtypeStruct((b, l, KV_WIDTH), BF16),
        jax.ShapeDtypeStruct((b, l, KV_WIDTH), BF16),
        jax.ShapeDtypeStruct((b, l, RET_WIDTH), BF16),
        jax.ShapeDtypeStruct((b, RET_WIDTH, l), BF16),
        jax.ShapeDtypeStruct((b, l, RET_WIDTH), BF16),
        jax.ShapeDtypeStruct((b, l, RET_WIDTH), F32),
        jax.ShapeDtypeStruct((b, l, CONV_WIDTH), F32),
    )
    out_specs = (
        pl.BlockSpec((1, tm, ATT_WIDTH), row),
        pl.BlockSpec((1, tm, KV_WIDTH), row),
        pl.BlockSpec((1, tm, KV_WIDTH), row),
        pl.BlockSpec((1, tm, RET_WIDTH), row),
        pl.BlockSpec((1, RET_WIDTH, tm), lambda i, j: (i, 0, j)),
        pl.BlockSpec((1, tm, RET_WIDTH), row),
        pl.BlockSpec((1, tm, RET_WIDTH), row),
        pl.BlockSpec((1, tm, CONV_WIDTH), row),
    )
    in_specs = [
        pl.BlockSpec((1, tm, d), row),
        pl.BlockSpec((1, d), const),
        pl.BlockSpec((1, 1, d), mod_map),
        pl.BlockSpec((1, 1, d), mod_map),
        pl.BlockSpec((d, IN_COLS), const),
        pl.BlockSpec((RET_WIDTH, d), const),
        pl.BlockSpec((tm, LANES), tab),
        pl.BlockSpec((tm, LANES), tab),
        pl.BlockSpec((tm, LANES), tab),
        pl.BlockSpec((tm, LANES), tab),
        pl.BlockSpec((HEAD_DIM // 2, tm), lambda i, j: (0, j)),
        pl.BlockSpec((HEAD_DIM // 2, tm), lambda i, j: (0, j)),
        pl.BlockSpec((1, LANES), const),
        pl.BlockSpec((1, LANES), const),
    ]
    return pl.pallas_call(
        _proj_kernel, out_shape=out_shape, grid=(b, l // tm), in_specs=in_specs, out_specs=out_specs,
        compiler_params=_cparams("arbitrary", "arbitrary"), name="in_proj",
    )(x, g, sh, sc, w_bf, wkt_bf, cosa, sina, cosr, sinr, cosk, sink, qg, kg)


def _softmax_pv(qs, pieces, sink_col):
    scores = []
    for k, _, mask in pieces:
        s = lax.dot_general(qs, k, (((1,), (1,)), ((), ())), preferred_element_type=F32)
        if mask is not None:
            s = jnp.where(mask, s, NEG_INF)
        scores.append(s)
    m_vec = None
    for s in scores:
        for j in range(s.shape[1] // LANES):
            chunk = s[:, j * LANES:(j + 1) * LANES]
            m_vec = chunk if m_vec is None else jnp.maximum(m_vec, chunk)
    m = jnp.maximum(sink_col, jnp.max(m_vec, axis=1, keepdims=True))
    p_vec = None
    acc = None
    for s, (_, v, _) in zip(scores, pieces):
        p = jnp.exp(s - m)
        for j in range(p.shape[1] // LANES):
            chunk = p[:, j * LANES:(j + 1) * LANES]
            p_vec = chunk if p_vec is None else p_vec + chunk
        o = jnp.dot(p.astype(BF16), v, preferred_element_type=F32)
        acc = o if acc is None else acc + o
    den = jnp.exp(sink_col - m) + jnp.sum(p_vec, axis=1, keepdims=True)
    return acc / den


def _sink_column(sink_ref, kh, rows_per_head):
    row = lax.broadcasted_iota(jnp.int32, (ATT_GROUP * rows_per_head, 1), 0)
    col = jnp.full(row.shape, sink_ref[kh * ATT_GROUP + ATT_GROUP - 1], F32)
    for g in range(ATT_GROUP - 2, -1, -1):
        col = jnp.where(row < (g + 1) * rows_per_head, sink_ref[kh * ATT_GROUP + g], col)
    return col


def _stack_heads(q_ref, kh):
    return jnp.concatenate(
        [q_ref[0, :, (kh * ATT_GROUP + g) * HEAD_DIM:(kh * ATT_GROUP + g + 1) * HEAD_DIM] for g in range(ATT_GROUP)],
        axis=0)


def _unstack_heads(o_ref, kh, out, rows_per_head):
    for g in range(ATT_GROUP):
        hd = kh * ATT_GROUP + g
        o_ref[0, :, hd * HEAD_DIM:(hd + 1) * HEAD_DIM] = out[g * rows_per_head:(g + 1) * rows_per_head].astype(BF16)


def _att_lat_kernel(sink_ref, q_ref, kp_ref, kc_ref, kn_ref, vp_ref, vc_ref, vn_ref, kx_ref, vx_ref, o_ref):
    n = pl.program_id(1)
    nb = pl.num_programs(1)
    shape = (ATT_GROUP * BLOCK, BLOCK)
    a = lax.broadcasted_iota(jnp.int32, shape, 0) & (BLOCK - 1)
    w = lax.broadcasted_iota(jnp.int32, shape, 1)
    far = 4 * BLOCK
    mask_p = w >= a + jnp.where(n > 0, 0, far)
    mask_n = w <= a - jnp.where(n < nb - 1, 0, far)
    for kh in range(ATT_KV_HEADS):
        sl = slice(kh * HEAD_DIM, (kh + 1) * HEAD_DIM)
        pieces = [(kp_ref[0, :, sl], vp_ref[0, :, sl], mask_p),
                  (kc_ref[0, :, sl], vc_ref[0, :, sl], None),
                  (kn_ref[0, :, sl], vn_ref[0, :, sl], mask_n),
                  (kx_ref[0, :, sl], vx_ref[0, :, sl], None)]
        out = _softmax_pv(_stack_heads(q_ref, kh), pieces, _sink_column(sink_ref, kh, BLOCK))
        _unstack_heads(o_ref, kh, out, BLOCK)


def _attention_latent(q, k, v, kx, vx, sink):
    b, s, _ = q.shape
    c = kx.shape[1]
    nb = s // BLOCK
    cur = lambda i, n: (i, n, 0)
    prv = lambda i, n: (i, jnp.maximum(n - 1, 0), 0)
    nxt = lambda i, n: (i, jnp.minimum(n + 1, nb - 1), 0)
    ctx = lambda i, n: (i, 0, 0)
    kv = lambda m: pl.BlockSpec((1, BLOCK, KV_WIDTH), m)
    return pl.pallas_call(
        _att_lat_kernel,
        out_shape=jax.ShapeDtypeStruct((b, s, ATT_WIDTH), BF16),
        grid=(b, nb),
        in_specs=[pl.BlockSpec(memory_space=pltpu.SMEM),
                  pl.BlockSpec((1, BLOCK, ATT_WIDTH), cur),
                  kv(prv), kv(cur), kv(nxt), kv(prv), kv(cur), kv(nxt),
                  pl.BlockSpec((1, c, KV_WIDTH), ctx), pl.BlockSpec((1, c, KV_WIDTH), ctx)],
        out_specs=pl.BlockSpec((1, BLOCK, ATT_WIDTH), cur),
        compiler_params=_cparams("arbitrary", "arbitrary"), name="att_latent",
    )(sink, q, k, k, k, v, v, v, kx, vx)


def _att_ctx_kernel(sink_ref, q_ref, k_ref, v_ref, o_ref):
    c = q_ref.shape[1]
    for kh in range(ATT_KV_HEADS):
        sl = slice(kh * HEAD_DIM, (kh + 1) * HEAD_DIM)
        out = _softmax_pv(_stack_heads(q_ref, kh), [(k_ref[0, :, sl], v_ref[0, :, sl], None)],
                          _sink_column(sink_ref, kh, c))
        _unstack_heads(o_ref, kh, out, c)


def _attention_context(q, k, v, sink):
    b, c, _ = q.shape
    blk = lambda w: pl.BlockSpec((1, c, w), lambda i: (i, 0, 0))
    return pl.pallas_call(
        _att_ctx_kernel,
        out_shape=jax.ShapeDtypeStruct((b, c, ATT_WIDTH), BF16),
        grid=(b,),
        in_specs=[pl.BlockSpec(memory_space=pltpu.SMEM), blk(ATT_WIDTH), blk(KV_WIDTH), blk(KV_WIDTH)],
        out_specs=blk(ATT_WIDTH),
        compiler_params=_cparams("arbitrary"), name="att_context",
    )(sink, q, k, v)


def _ret_scan_kernel(lg_ref, s0_ref, qf_ref, ktf_ref, vf_ref, qb_ref, ktb_ref, vb_ref,
                     of_ref, ob_ref, sfin_ref, st_s):
    c = pl.program_id(1)
    t = RET_CHUNK

    @pl.when(c == 0)
    def _():
        st_s[...] = s0_ref[0]

    i_col = lax.broadcasted_iota(jnp.int32, (t, HEAD_DIM), 0).astype(F32)
    j_row = lax.broadcasted_iota(jnp.int32, (HEAD_DIM, t), 1).astype(F32)
    ii = lax.broadcasted_iota(jnp.int32, (t, t), 0)
    jj = lax.broadcasted_iota(jnp.int32, (t, t), 1)
    dist = (ii - jj).astype(F32)
    for hd in range(RET_HEADS):
        lgf = lg_ref[0, hd]
        lgb = lg_ref[1, hd]
        sl = slice(hd * HEAD_DIM, (hd + 1) * HEAD_DIM)
        q = qf_ref[0, :, sl]
        kt = ktf_ref[0, sl, :]
        v = vf_ref[0, :, sl]
        dec = jnp.where(ii >= jj, jnp.exp(lgf * jnp.maximum(dist, 0.0)), jnp.exp(lgb * jnp.maximum(-dist, 0.0)))
        s = jnp.dot(q, kt, preferred_element_type=F32) * dec
        o = jnp.dot(s.astype(BF16), v, preferred_element_type=F32)
        st = st_s[0, hd]
        qd = (q.astype(F32) * jnp.exp(lgf * (i_col + 1.0))).astype(BF16)
        of_ref[0, :, sl] = o + jnp.dot(qd, st.astype(BF16), preferred_element_type=F32)
        kd = (kt.astype(F32) * jnp.exp(lgf * (t - 1.0 - j_row))).astype(BF16)
        st_s[0, hd] = jnp.exp(lgf * t) * st + jnp.dot(kd, v, preferred_element_type=F32)
        q = qb_ref[0, :, sl]
        kt = ktb_ref[0, sl, :]
        v = vb_ref[0, :, sl]
        st = st_s[1, hd]
        qd = (q.astype(F32) * jnp.exp(lgb * (t - i_col))).astype(BF16)
        ob_ref[0, :, sl] = jnp.dot(qd, st.astype(BF16), preferred_element_type=F32)
        kd = (kt.astype(F32) * jnp.exp(lgb * j_row)).astype(BF16)
        st_s[1, hd] = jnp.exp(lgb * t) * st + jnp.dot(kd, v, preferred_element_type=F32)

    @pl.when(c == pl.num_programs(1) - 1)
    def _():
        sfin_ref[0] = st_s[...]


def _retention_scan(q, kt, v, s0, log_g):
    b, l, _ = q.shape
    n = l // RET_CHUNK
    fwd = lambda i, c: (i, c, 0)
    bwd = lambda i, c: (i, n - 1 - c, 0)
    fwd_t = lambda i, c: (i, 0, c)
    bwd_t = lambda i, c: (i, 0, n - 1 - c)
    st_map = lambda i, c: (i, 0, 0, 0, 0)
    tok = lambda m: pl.BlockSpec((1, RET_CHUNK, RET_WIDTH), m)
    feat = lambda m: pl.BlockSpec((1, RET_WIDTH, RET_CHUNK), m)
    st_blk = pl.BlockSpec((1, 2, RET_HEADS, HEAD_DIM, HEAD_DIM), st_map)
    return pl.pallas_call(
        _ret_scan_kernel,
        out_shape=(jax.ShapeDtypeStruct((b, l, RET_WIDTH), F32),
                   jax.ShapeDtypeStruct((b, l, RET_WIDTH), F32),
                   jax.ShapeDtypeStruct((b, 2, RET_HEADS, HEAD_DIM, HEAD_DIM), F32)),
        grid=(b, n),
        in_specs=[pl.BlockSpec(memory_space=pltpu.SMEM), st_blk,
                  tok(fwd), feat(fwd_t), tok(fwd), tok(bwd), feat(bwd_t), tok(bwd)],
        out_specs=(tok(fwd), tok(bwd), st_blk),
        scratch_shapes=[pltpu.VMEM((2, RET_HEADS, HEAD_DIM, HEAD_DIM), F32)],
        compiler_params=_cparams("arbitrary", "arbitrary"), name="retention_scan",
    )(log_g, s0, q, kt, v, q, kt, v)


def _pair_group_norm_gate(o, gate, g):
    lane = lax.broadcasted_iota(jnp.int32, o.shape, 1)
    lo = lane < HEAD_DIM

    def pair_mean(z):
        s_all = jnp.sum(z, axis=1, keepdims=True)
        s_lo = jnp.sum(jnp.where(lo, z, 0.0), axis=1, keepdims=True)
        return jnp.where(lo, s_lo, s_all - s_lo) * (1.0 / HEAD_DIM)

    cen = o - pair_mean(o)
    var = pair_mean(cen * cen)
    y = cen * lax.rsqrt(var + EPS) * g
    return (y * _silu(gate)).astype(BF16)


CONV_HALO = 16
CONV_ROWS = 128
CONV_COPY_ROWS = 256


def _conv_kernel(h_ref, w_ref, b_ref, g_ref, beta_ref, o_ref, pad_s, shift_s):
    l = h_ref.shape[1]
    zeros = jnp.zeros((CONV_HALO, CONV_WIDTH), F32)
    pad_s[0:CONV_HALO, :] = zeros
    pad_s[CONV_HALO + l:2 * CONV_HALO + l, :] = zeros
    pad_s[CONV_HALO:CONV_HALO + l, :] = h_ref[0]
    ext = shift_s.shape[1]
    for r in range(1, SUBLANES):
        for c0 in range(0, ext, CONV_COPY_ROWS):
            n = min(CONV_COPY_ROWS, ext - c0)
            shift_s[r - 1, c0:c0 + n, :] = pad_s[c0 + r:c0 + r + n, :]
    for r0 in range(0, l, CONV_ROWS):
        acc = jnp.zeros((CONV_ROWS, CONV_WIDTH), F32) + b_ref[...]
        for j in range(CONV_KERNEL):
            off = j + CONV_HALO - CONV_PAD
            start = r0 + (off // SUBLANES) * SUBLANES
            if off % SUBLANES == 0:
                rows = pad_s[start:start + CONV_ROWS, :]
            else:
                rows = shift_s[off % SUBLANES - 1, start:start + CONV_ROWS, :]
            acc = acc + rows * w_ref[j:j + 1, :]
        mu = jnp.mean(acc, axis=1, keepdims=True)
        cen = acc - mu
        var = jnp.mean(cen * cen, axis=1, keepdims=True)
        y = cen * lax.rsqrt(var + EPS) * g_ref[...] + beta_ref[...]
        o_ref[0, r0:r0 + CONV_ROWS, :] = _silu(y).astype(BF16)


def _conformer_conv(hc, w, bias, g, beta):
    b, l, cw = hc.shape
    blk = pl.BlockSpec((1, l, cw), lambda i: (i, 0, 0))
    vec = pl.BlockSpec((1, cw), lambda i: (0, 0))
    return pl.pallas_call(
        _conv_kernel,
        out_shape=jax.ShapeDtypeStruct((b, l, cw), BF16),
        grid=(b,),
        in_specs=[blk, pl.BlockSpec((CONV_KERNEL, cw), lambda i: (0, 0)), vec, vec, vec],
        out_specs=blk,
        scratch_shapes=[pltpu.VMEM((l + 2 * CONV_HALO, cw), F32),
                        pltpu.VMEM((SUBLANES - 1, l + 2 * CONV_HALO - SUBLANES, cw), F32)],
        compiler_params=_cparams("arbitrary"), name="conformer_conv",
    )(hc, w, bias, g, beta)


OUT_SUB_ROWS = 256


def _out_kernel(att_ref, of_ref, ob_ref, rgate_ref, gn_ref, conv_ref, w_ref, x_ref, gt_ref, g_ref, sh_ref, sc_ref,
                wr_ref, xmid_ref, hf_ref, aff_ref):
    tm = x_ref.shape[1]
    sub = min(OUT_SUB_ROWS, tm)
    for i in range(tm // sub):
        rows = slice(i * sub, (i + 1) * sub)
        mix = jnp.dot(att_ref[0, rows, :], w_ref[0:ATT_WIDTH, :], preferred_element_type=F32)
        for j in range(RET_WIDTH // LANES):
            sl = slice(j * LANES, (j + 1) * LANES)
            ret = _pair_group_norm_gate(of_ref[0, rows, sl] + ob_ref[0, rows, sl], rgate_ref[0, rows, sl],
                                        gn_ref[:, sl])
            mix = mix + jnp.dot(ret, w_ref[ATT_WIDTH + j * LANES:ATT_WIDTH + (j + 1) * LANES, :],
                                preferred_element_type=F32)
        mix = mix + jnp.dot(conv_ref[0, rows, :], w_ref[ATT_WIDTH + RET_WIDTH:, :], preferred_element_type=F32)
        xm = x_ref[0, rows, :] + gt_ref[0] * mix
        xmid_ref[0, rows, :] = xm
        ms = jnp.mean(xm * xm, axis=-1, keepdims=True)
        h = xm * lax.rsqrt(ms + EPS) * g_ref[...]
        h = h * (1.0 + sc_ref[0]) + sh_ref[0]
        hf_ref[0, rows, :] = h.astype(BF16)
        lt = lax.dot_general(wr_ref[...], h, (((1,), (1,)), ((), ())), precision=HIGHEST,
                             preferred_element_type=F32)
        e = jnp.exp(lt - jnp.max(lt, axis=0, keepdims=True))
        aff_ref[0, :, rows] = e / jnp.sum(e, axis=0, keepdims=True)


def _out_project(att, o_f, o_b, rgate, gn, conv, w_bf, x, gt, g, sh, sc, wr_t):
    b, l, d = x.shape
    tm = min(512, l)
    mod_map = (lambda i, j: (i, 0, 0)) if gt.shape[0] == b else (lambda i, j: (0, 0, 0))
    row = lambda i, j: (i, j, 0)
    const = lambda i, j: (0, 0)
    mod = pl.BlockSpec((1, 1, d), mod_map)
    return pl.pallas_call(
        _out_kernel,
        out_shape=(jax.ShapeDtypeStruct((b, l, d), F32),
                   jax.ShapeDtypeStruct((b, l, d), BF16),
                   jax.ShapeDtypeStruct((b, N_EXPERTS, l), F32)),
        grid=(b, l // tm),
        in_specs=[pl.BlockSpec((1, tm, ATT_WIDTH), row), pl.BlockSpec((1, tm, RET_WIDTH), row),
                  pl.BlockSpec((1, tm, RET_WIDTH), row), pl.BlockSpec((1, tm, RET_WIDTH), row),
                  pl.BlockSpec((1, RET_WIDTH), const),
                  pl.BlockSpec((1, tm, CONV_WIDTH), row), pl.BlockSpec((d, d), const),
                  pl.BlockSpec((1, tm, d), row), mod, pl.BlockSpec((1, d), const), mod, mod,
                  pl.BlockSpec((N_EXPERTS, d), const)],
        out_specs=(pl.BlockSpec((1, tm, d), row), pl.BlockSpec((1, tm, d), row),
                   pl.BlockSpec((1, N_EXPERTS, tm), lambda i, j: (i, 0, j))),
        compiler_params=_cparams("arbitrary", "arbitrary"), name="out_proj",
    )(att, o_f, o_b, rgate, gn, conv, w_bf, x, gt, g, sh, sc, wr_t)


def _prefix_sum_lanes(x):
    l = x.shape[1]
    lane = lax.broadcasted_iota(jnp.int32, x.shape, 1)
    k = 1
    while k < l:
        x = x + jnp.where(lane >= k, pltpu.roll(x, k, 1), 0.0)
        k *= 2
    return x


def _select_kernel(aff_ref, pos_ref, post_ref, *, cap):
    for s in range(aff_ref.shape[0]):
        _select_one(aff_ref.at[s], pos_ref.at[s], post_ref.at[s], cap)


def _select_one(aff_ref, pos_ref, post_ref, cap):
    aff = aff_ref[...]
    thr = jnp.zeros((aff.shape[0], 1), jnp.int32)
    for bit in range(30, -1, -1):
        cand = thr | (1 << bit)
        cnt = jnp.sum(jnp.where(aff >= pltpu.bitcast(cand, F32), 1.0, 0.0), axis=1, keepdims=True)
        thr = jnp.where(cnt >= cap, cand, thr)
    above = aff >= pltpu.bitcast(thr + 1, F32)
    tie = (aff >= pltpu.bitcast(thr, F32)) & jnp.logical_not(above)
    n_above = jnp.sum(jnp.where(above, 1.0, 0.0), axis=1, keepdims=True)
    tie_f = jnp.where(tie, 1.0, 0.0)
    tie_rank = _prefix_sum_lanes(tie_f) - tie_f
    sel = above | (tie & (tie_rank < cap - n_above))
    sel_f = jnp.where(sel, 1.0, 0.0)
    pos = jnp.where(sel, _prefix_sum_lanes(sel_f) - sel_f, -1.0)
    pos_ref[...] = pos
    padded = jnp.concatenate([pos, jnp.full((LANES - pos.shape[0], pos.shape[1]), -1.0, F32)], axis=0)
    post_ref[...] = padded.T


SELECT_SAMPLES = 4


def _select(aff_t, cap):
    b, e, l = aff_t.shape
    sb = SELECT_SAMPLES if b % SELECT_SAMPLES == 0 else 1
    return pl.pallas_call(
        functools.partial(_select_kernel, cap=cap),
        out_shape=(jax.ShapeDtypeStruct((b, e, l), F32), jax.ShapeDtypeStruct((b, l, LANES), F32)),
        grid=(b // sb,),
        in_specs=[pl.BlockSpec((sb, e, l), lambda i: (i, 0, 0))],
        out_specs=(pl.BlockSpec((sb, e, l), lambda i: (i, 0, 0)), pl.BlockSpec((sb, l, LANES), lambda i: (i, 0, 0))),
        compiler_params=_cparams("arbitrary"), name="expert_select",
    )(aff_t)


def _gather_kernel(h_ref, pos_ref, aff_ref, x_ref, g_ref, *, cap):
    l = h_ref.shape[1]
    ge = pos_ref.shape[1]
    slot = lax.broadcasted_iota(jnp.int32, (cap, l), 0).astype(F32)
    hits = [pos_ref[0, el] == slot for el in range(ge)]
    onehot = jnp.concatenate([jnp.where(hit, 1.0, 0.0).astype(BF16) for hit in hits], axis=0)
    x = jnp.dot(onehot, h_ref[0], preferred_element_type=F32).astype(BF16)
    for el, hit in enumerate(hits):
        x_ref[el] = x[el * cap:(el + 1) * cap]
        g_ref[el] = jnp.sum(jnp.where(hit, aff_ref[0, el], 0.0), axis=1, keepdims=True)


GATHER_ROWS = 512


def _gather(hf, pos, aff_t, cap):
    b, l, d = hf.shape
    e = pos.shape[1]
    ge = max(1, min(e, GATHER_ROWS // cap))
    row = pl.BlockSpec((1, ge, 1, l), lambda i, j: (i, j, 0, 0))
    return pl.pallas_call(
        functools.partial(_gather_kernel, cap=cap),
        out_shape=(jax.ShapeDtypeStruct((e, b * cap, d), BF16), jax.ShapeDtypeStruct((e, b * cap, 1), F32)),
        grid=(b, e // ge),
        in_specs=[pl.BlockSpec((1, l, d), lambda i, j: (i, 0, 0)), row, row],
        out_specs=(pl.BlockSpec((ge, cap, d), lambda i, j: (j, i, 0)), pl.BlockSpec((ge, cap, 1), lambda i, j: (j, i, 0))),
        compiler_params=_cparams("arbitrary", "arbitrary"), name="expert_gather",
    )(hf, pos.reshape(b, e, 1, l), aff_t.reshape(b, e, 1, l))


FFN_ROWS = 512
FFN_M_TILES = 2
FFN_F_TILE = 256


def _ffn_kernel(*refs, n_sets):
    wg_ref, wu_ref, wd_ref = refs[:3]
    x_refs = refs[3:3 + n_sets]
    g_refs = refs[3 + n_sets:3 + 2 * n_sets]
    y_refs = refs[3 + 2 * n_sets:3 + 3 * n_sets]
    acc_refs = refs[3 + 3 * n_sets:]
    f = pl.program_id(2)
    wg = wg_ref[0].astype(BF16)
    wu = wu_ref[0].astype(BF16)
    wd = wd_ref[0].astype(BF16)

    @pl.when(f == 0)
    def _():
        for acc_s in acc_refs:
            acc_s[...] = jnp.zeros_like(acc_s)

    for x_ref, acc_s in zip(x_refs, acc_refs):
        tm = x_ref.shape[1]
        rc = min(FFN_ROWS, tm)
        for i in range(tm // rc):
            rows = slice(i * rc, (i + 1) * rc)
            x = x_ref[0, rows, :]
            a = jnp.dot(x, wg, preferred_element_type=F32)
            u = jnp.dot(x, wu, preferred_element_type=F32)
            hm = (_silu(a) * u).astype(BF16)
            acc_s[rows, :] += jnp.dot(hm, wd, preferred_element_type=F32)

    @pl.when(f == pl.num_programs(2) - 1)
    def _():
        for acc_s, g_ref, y_ref in zip(acc_refs, g_refs, y_refs):
            y_ref[0] = (acc_s[...] * g_ref[0]).astype(BF16)


def _expert_ffn(xes, gates, w_gate, w_up, w_down, layer):
    e, _, d = xes[0].shape
    ff = w_gate.shape[-1]
    tf = FFN_F_TILE
    tms = [xe.shape[1] // FFN_M_TILES for xe in xes]
    row = lambda i, j, f: (i, j, 0)
    w_in_blk = pl.BlockSpec((None, 1, d, tf), lambda i, j, f: (layer, i, 0, f))
    return pl.pallas_call(
        functools.partial(_ffn_kernel, n_sets=len(xes)),
        out_shape=[jax.ShapeDtypeStruct(xe.shape, BF16) for xe in xes],
        grid=(e, FFN_M_TILES, ff // tf),
        in_specs=([w_in_blk, w_in_blk, pl.BlockSpec((None, 1, tf, d), lambda i, j, f: (layer, i, f, 0))]
                  + [pl.BlockSpec((1, tm, d), row) for tm in tms]
                  + [pl.BlockSpec((1, tm, 1), row) for tm in tms]),
        out_specs=[pl.BlockSpec((1, tm, d), row) for tm in tms],
        scratch_shapes=[pltpu.VMEM((tm, d), F32) for tm in tms],
        compiler_params=_cparams("arbitrary", "arbitrary", "arbitrary"), name="expert_ffn",
    )(w_gate, w_up, w_down, *xes, *gates)


def _combine_kernel(y_ref, post_ref, x_ref, gt_ref, o_ref, *, cap):
    tt = x_ref.shape[1]
    slot = lax.broadcasted_iota(jnp.int32, (tt, cap), 1).astype(F32)
    acc = jnp.zeros((tt, x_ref.shape[2]), F32)
    for ex in range(N_EXPERTS):
        onehot = jnp.where(post_ref[0, :, ex:ex + 1] == slot, 1.0, 0.0).astype(BF16)
        acc = acc + jnp.dot(onehot, y_ref[ex], preferred_element_type=F32)
    o_ref[0] = x_ref[0] + gt_ref[0] * acc


def _combine(y, pos_t, x_mid, gt, cap):
    b, l, d = x_mid.shape
    e = y.shape[0]
    tt = min(512, l)
    mod_map = (lambda i, j: (i, 0, 0)) if gt.shape[0] == b else (lambda i, j: (0, 0, 0))
    return pl.pallas_call(
        functools.partial(_combine_kernel, cap=cap),
        out_shape=jax.ShapeDtypeStruct((b, l, d), F32),
        grid=(b, l // tt),
        in_specs=[pl.BlockSpec((e, cap, d), lambda i, j: (0, i, 0)),
                  pl.BlockSpec((1, tt, LANES), lambda i, j: (i, j, 0)),
                  pl.BlockSpec((1, tt, d), lambda i, j: (i, j, 0)),
                  pl.BlockSpec((1, 1, d), mod_map)],
        out_specs=pl.BlockSpec((1, tt, d), lambda i, j: (i, j, 0)),
        compiler_params=_cparams("arbitrary", "arbitrary"), name="expert_combine",
    )(y, pos_t, x_mid, gt)


def _rotary_tables(pos, half):
    inv = ROPE_BASE ** (-np.arange(half, dtype=np.float64) / half)
    ang = pos.astype(np.float64)[:, None] * inv[None, :]
    return np.cos(ang), np.sin(ang)


def _position_tables(seq, ctx_len):
    f = lambda a: jnp.asarray(a, F32)
    t = np.arange(seq)
    cr, sr = _rotary_tables(t // GRID_W, HEAD_DIM // 4)
    cc, sc = _rotary_tables(t % GRID_W, HEAD_DIM // 4)
    cos_a = np.tile(np.concatenate([cr, cr, cc, cc], axis=1), (1, LANES // HEAD_DIM))
    sin_a = np.tile(np.concatenate([-sr, sr, -sc, sc], axis=1), (1, LANES // HEAD_DIM))

    def ret(pos):
        c, s = _rotary_tables(pos, HEAD_DIM // 2)
        return (f(np.tile(np.concatenate([c, c], axis=1), (1, LANES // HEAD_DIM))),
                f(np.tile(np.concatenate([-s, s], axis=1), (1, LANES // HEAD_DIM))),
                f(c.T), f(s.T))

    lat = (f(cos_a), f(sin_a)) + ret(ctx_len + np.arange(seq))
    ctx = (jnp.ones((ctx_len, LANES), F32), jnp.zeros((ctx_len, LANES), F32)) + ret(np.arange(ctx_len))
    return lat, ctx


def _route(hf, aff_t):
    cap = CAPACITY_FACTOR * hf.shape[1] // N_EXPERTS
    pos, pos_t = _select(aff_t, cap)
    xe, gate = _gather(hf, pos, aff_t, cap)
    return cap, pos_t, xe, gate


def kernel(x, c, ctx, c_ctx, w_mod, b_mod, g_mix, g_ffn, w_in, q_norm_g, k_norm_g, att_sink, ret_decay_logit,
           ret_norm_g, conv_w, conv_b, conv_norm_g, conv_norm_b, w_out, w_router, w_gate, w_up, w_down):
    b, s, d = x.shape
    cl = ctx.shape[1]
    depth = w_mod.shape[0]
    tabs_l, tabs_c = _position_tables(s, cl)

    cvec = jnp.concatenate([c, c_ctx[None, :], jnp.zeros((MOD_ROWS - b - 1, d), F32)], axis=0)
    mod = _modulation(cvec, w_mod, b_mod)

    w_in_bf = w_in.astype(BF16)
    w_rk_t = jnp.swapaxes(w_in[:, :, COL_RK:COL_RK + RET_WIDTH], 1, 2).astype(BF16)
    w_out_bf = w_out.astype(BF16)
    w_router_t = jnp.swapaxes(w_router, 1, 2)
    log_g = jax.nn.log_sigmoid(ret_decay_logit.astype(F32))
    tile2 = lambda g: jnp.tile(g, LANES // HEAD_DIM)[None, :]
    zero_state = jnp.zeros((b, 2, RET_HEADS, HEAD_DIM, HEAD_DIM), F32)

    xc = ctx
    for l in range(depth):
        last = l == depth - 1
        m_l = [mod[l, :b, i * d:(i + 1) * d][:, None, :] for i in range(6)]
        m_c = [mod[l, b:b + 1, i * d:(i + 1) * d][:, None, :] for i in range(6)]
        vec = lambda a: a[l][None, :]
        qg, kg = tile2(q_norm_g[l]), tile2(k_norm_g[l])

        aq_c, ak_c, av_c, rq_c, rkt_c, rv_c, rg_c, hc_c = _project(
            xc, vec(g_mix), m_c[0], m_c[1], w_in_bf[l], w_rk_t[l], tabs_c, qg, kg)
        aq_l, ak_l, av_l, rq_l, rkt_l, rv_l, rg_l, hc_l = _project(
            x, vec(g_mix), m_l[0], m_l[1], w_in_bf[l], w_rk_t[l], tabs_l, qg, kg)

        att_l = _attention_latent(aq_l, ak_l, av_l, ak_c, av_c, att_sink[l])
        of_c, ob_c, s_ctx = _retention_scan(rq_c, rkt_c, rv_c, zero_state, log_g[l])
        of_l, ob_l, _ = _retention_scan(rq_l, rkt_l, rv_l, s_ctx, log_g[l])
        conv_l = _conformer_conv(hc_l, conv_w[l], vec(conv_b), vec(conv_norm_g), vec(conv_norm_b))
        x_mid, hf_l, aff_l = _out_project(att_l, of_l, ob_l, rg_l, vec(ret_norm_g), conv_l, w_out_bf[l], x,
                                          m_l[2], vec(g_ffn), m_l[3], m_l[4], w_router_t[l])
        routed = [_route(hf_l, aff_l)]
        if not last:
            att_c = _attention_context(aq_c, ak_c, av_c, att_sink[l])
            conv_c = _conformer_conv(hc_c, conv_w[l], vec(conv_b), vec(conv_norm_g), vec(conv_norm_b))
            xc_mid, hf_c, aff_c = _out_project(att_c, of_c, ob_c, rg_c, vec(ret_norm_g), conv_c, w_out_bf[l], xc,
                                               m_c[2], vec(g_ffn), m_c[3], m_c[4], w_router_t[l])
            routed.append(_route(hf_c, aff_c))
        ys = _expert_ffn([r[2] for r in routed], [r[3] for r in routed], w_gate, w_up, w_down, l)
        x = _combine(ys[0], routed[0][1], x_mid, m_l[5], routed[0][0])
        if not last:
            xc = _combine(ys[1], routed[1][1], xc_mid, m_c[5], routed[1][0])
    return x
```
